```python
import math
import functools
import jax
import jax.numpy as jnp
from jax import lax
import numpy as np


D_MODEL = 1024
BATCH = 8
SEQ = 2048
DEPTH = 1
DEC_BATCH = 128
DEC_SEQ = 4
PAST_LEN = 16384
PAGE_SIZE = 128

D_FF = 2816
EPS = 1e-6
D_SSM = D_MODEL // 2
SSM_GROUP = 16
N_SSM_GROUPS = D_SSM // SSM_GROUP
SSM_STATE = 64
DT_MIN = 1e-3
DT_MAX = 1e-1
N_HEADS = 8
QK_NOPE = 64
QK_ROPE = 32
V_DIM = 64
D_ATT = N_HEADS * V_DIM
Q_LORA = 256
KV_LORA = 256
ROPE_THETA = 10000.0
SOFTMAX_SCALE = 1.0 / math.sqrt(QK_NOPE + QK_ROPE)
Q_BLOCK = 128
OFF_U = 0
OFF_Q = OFF_U + D_SSM
OFF_KV = OFF_Q + Q_LORA
OFF_PE = OFF_KV + KV_LORA
D_IN = OFF_PE + QK_ROPE
D_MIX = D_SSM + D_ATT

kernel_name = 'hymba_s5_mla_macaron_step'


def rmsnorm(x, g):
    xf = x.astype(jnp.float32)
    y = xf * lax.rsqrt(jnp.mean(xf * xf, axis=-1, keepdims=True) + EPS)
    return (y * g.astype(jnp.float32)).astype(x.dtype)


def macaron_ffn(x, pre_g, w_gate, w_up, w_down, post_g):
    h = rmsnorm(x, pre_g)
    f = (jax.nn.silu(h @ w_gate) * (h @ w_up)) @ w_down
    return x + 0.5 * rmsnorm(f, post_g)


def rope_angles(pos):
    inv = ROPE_THETA ** (-jnp.arange(0, QK_ROPE, 2, dtype=jnp.float32) / QK_ROPE)
    ang = pos[:, None] * inv[None, :]
    return jnp.cos(ang), jnp.sin(ang)


def apply_rope(x, cos, sin):
    half = QK_ROPE // 2
    bshape = (x.shape[1],) + (1,) * (x.ndim - 3) + (half,)
    cos = cos.reshape(bshape)
    sin = sin.reshape(bshape)
    xf = x.astype(jnp.float32)
    x1, x2 = xf[..., :half], xf[..., half:]
    return jnp.concatenate([x1 * cos - x2 * sin, x2 * cos + x1 * sin], axis=-1).astype(x.dtype)


def cmul(ar, ai, br, bi):
    return ar * br - ai * bi, ar * bi + ai * br


def zoh_discretise(lam_re, lam_im, log_step):
    dt = jnp.exp(log_step.astype(jnp.float32))[:, None]
    lr = lam_re.astype(jnp.float32)
    li = lam_im.astype(jnp.float32)
    mag = jnp.exp(lr * dt)
    a_re = mag * jnp.cos(li * dt)
    a_im = mag * jnp.sin(li * dt)
    den = lr * lr + li * li
    n_re = a_re - 1.0
    z_re = (n_re * lr + a_im * li) / den
    z_im = (a_im * lr - n_re * li) / den
    return a_re, a_im, z_re, z_im


def _ssm_combine(e1, e2):
    a1r, a1i, b1r, b1i = e1
    a2r, a2i, b2r, b2i = e2
    ar, ai = cmul(a2r, a2i, a1r, a1i)
    br, bi = cmul(a2r, a2i, b1r, b1i)
    return ar, ai, br + b2r, bi + b2i


def s5_mixer(u, h0_re, h0_im, lam_re, lam_im, log_step, b_re, b_im, c_re, c_im, d_skip, w_glu, b_glu):
    bsz, t = u.shape[0], u.shape[1]
    f32 = jnp.float32
    uf = u.astype(f32)
    ug = uf.reshape(bsz, t, N_SSM_GROUPS, SSM_GROUP)
    a_re, a_im, z_re, z_im = zoh_discretise(lam_re, lam_im, log_step)
    bb_re, bb_im = cmul(z_re[..., None], z_im[..., None], b_re.astype(f32), b_im.astype(f32))
    bu_re = jnp.einsum('btgc,gpc->btgp', ug, bb_re)
    bu_im = jnp.einsum('btgc,gpc->btgp', ug, bb_im)
    carry_re, carry_im = cmul(a_re, a_im, h0_re.astype(f32), h0_im.astype(f32))
    bu_re = bu_re.at[:, 0].add(carry_re)
    bu_im = bu_im.at[:, 0].add(carry_im)
    elems = (jnp.broadcast_to(a_re, bu_re.shape), jnp.broadcast_to(a_im, bu_im.shape), bu_re, bu_im)
    _, _, h_re, h_im = lax.associative_scan(_ssm_combine, elems, axis=1)
    y = (jnp.einsum('btgp,gcp->btgc', h_re, c_re.astype(f32))
         - jnp.einsum('btgp,gcp->btgc', h_im, c_im.astype(f32)))
    y = y.reshape(bsz, t, D_SSM) + d_skip.astype(f32) * uf
    y = jax.nn.gelu(y)
    y = y * jax.nn.sigmoid(y @ w_glu.astype(f32) + b_glu.astype(f32))
    return y.astype(u.dtype), h_re[:, -1], h_im[:, -1]


def latent_scores(q_lat, q_pe, c_kv, k_pe):
    s = jnp.einsum('bthr,bsr->bhts', q_lat, c_kv, preferred_element_type=jnp.float32)
    s = s + jnp.einsum('bthe,bse->bhts', q_pe, k_pe, preferred_element_type=jnp.float32)
    return s * SOFTMAX_SCALE


def mla_prompt_attend(q_lat, q_pe, c_kv, k_pe):
    bsz, t = q_lat.shape[0], q_lat.shape[1]
    nb = t // Q_BLOCK
    ql = q_lat.reshape(bsz, nb, Q_BLOCK, N_HEADS, KV_LORA).swapaxes(0, 1)
    qp = q_pe.reshape(bsz, nb, Q_BLOCK, N_HEADS, QK_ROPE).swapaxes(0, 1)
    k_pos = jnp.arange(t)

    def block(args):
        i, ql_b, qp_b = args
        s = latent_scores(ql_b, qp_b, c_kv, k_pe)
        q_pos = i * Q_BLOCK + jnp.arange(Q_BLOCK)
        s = jnp.where(k_pos[None, :] <= q_pos[:, None], s, -jnp.inf)
        p = jax.nn.softmax(s, axis=-1).astype(c_kv.dtype)
        return jnp.einsum('bhts,bsr->bthr', p, c_kv)

    o = lax.map(block, (jnp.arange(nb), ql, qp))
    return o.swapaxes(0, 1).reshape(bsz, t, N_HEADS, KV_LORA)


def mla_sample_attend(q_lat, q_pe, c_new, pe_new, cache_ckv, cache_kpe, page_table, layer):
    bsz, t = q_lat.shape[0], q_lat.shape[1]
    c_past = cache_ckv[layer, page_table].reshape(bsz, -1, KV_LORA)
    pe_past = cache_kpe[layer, page_table].reshape(bsz, -1, QK_ROPE)
    n_past = c_past.shape[1]
    s_past = latent_scores(q_lat, q_pe, c_past, pe_past)
    s_new = latent_scores(q_lat, q_pe, c_new, pe_new)
    s_new = jnp.where(jnp.tril(jnp.ones((t, t), dtype=bool)), s_new, -jnp.inf)
    p = jax.nn.softmax(jnp.concatenate([s_past, s_new], axis=-1), axis=-1).astype(c_new.dtype)
    return (jnp.einsum('bhts,bsr->bthr', p[..., :n_past], c_past)
            + jnp.einsum('bhts,bsr->bthr', p[..., n_past:], c_new))


def setup_inputs(seed: int = 0) -> dict:
    key = jax.random.key(seed)
    ks = iter(jax.random.split(key, 64))
    f32 = jnp.float32

    def nrm(shape, scale):
        return scale * jax.random.normal(next(ks), shape, f32)

    def gain(dim):
        return 1.0 + nrm((DEPTH, dim), 0.01)

    n_pages = PAST_LEN // PAGE_SIZE
    n_pool = (DEC_BATCH * n_pages * 5) // 4
    x_prompt = nrm((BATCH, SEQ, D_MODEL), 1.0)
    x_sample = nrm((DEC_BATCH, DEC_SEQ, D_MODEL), 1.0)
    cache_ckv = nrm((DEPTH, n_pool, PAGE_SIZE, KV_LORA), 1.0)
    cache_kpe = nrm((DEPTH, n_pool, PAGE_SIZE, QK_ROPE), 1.0)
    state_ssm_re = nrm((DEPTH, DEC_BATCH, N_SSM_GROUPS, SSM_STATE), 0.3)
    state_ssm_im = nrm((DEPTH, DEC_BATCH, N_SSM_GROUPS, SSM_STATE), 0.3)
    page_table = jax.random.permutation(next(ks), n_pool)[: DEC_BATCH * n_pages]
    page_table = page_table.reshape(DEC_BATCH, n_pages).astype(jnp.int32)
    lam_shape = (DEPTH, N_SSM_GROUPS, SSM_STATE)
    ssm_lam_re = -0.5 + nrm(lam_shape, 0.01)
    ssm_lam_im = jnp.pi * jnp.arange(SSM_STATE, dtype=f32) + nrm(lam_shape, 0.01)
    ssm_log_step = jax.random.uniform(next(ks), (DEPTH, N_SSM_GROUPS), f32,
                                      math.log(DT_MIN), math.log(DT_MAX))
    return {
        'x_prompt': x_prompt,
        'x_sample': x_sample,
        'cache_ckv': cache_ckv,
        'cache_kpe': cache_kpe,
        'state_ssm_re': state_ssm_re,
        'state_ssm_im': state_ssm_im,
        'page_table': page_table,
        'ffn1_pre_g': gain(D_MODEL),
        'ffn1_w_gate': nrm((DEPTH, D_MODEL, D_FF), D_MODEL ** -0.5),
        'ffn1_w_up': nrm((DEPTH, D_MODEL, D_FF), D_MODEL ** -0.5),
        'ffn1_w_down': nrm((DEPTH, D_FF, D_MODEL), D_FF ** -0.5),
        'ffn1_post_g': gain(D_MODEL),
        'mix_pre_g': gain(D_MODEL),
        'w_in': nrm((DEPTH, D_MODEL, D_IN), D_MODEL ** -0.5),
        'ssm_lam_re': ssm_lam_re,
        'ssm_lam_im': ssm_lam_im,
        'ssm_log_step': ssm_log_step,
        'ssm_b_re': nrm((DEPTH, N_SSM_GROUPS, SSM_STATE, SSM_GROUP), (2 * SSM_GROUP) ** -0.5),
        'ssm_b_im': nrm((DEPTH, N_SSM_GROUPS, SSM_STATE, SSM_GROUP), (2 * SSM_GROUP) ** -0.5),
        'ssm_c_re': nrm((DEPTH, N_SSM_GROUPS, SSM_GROUP, SSM_STATE), (2 * SSM_STATE) ** -0.5),
        'ssm_c_im': nrm((DEPTH, N_SSM_GROUPS, SSM_GROUP, SSM_STATE), (2 * SSM_STATE) ** -0.5),
        'ssm_d': nrm((DEPTH, D_SSM), 1.0),
        'ssm_w_glu': nrm((DEPTH, D_SSM, D_SSM), D_SSM ** -0.5),
        'ssm_b_glu': nrm((DEPTH, D_SSM), 0.01),
        'q_norm_g': gain(Q_LORA),
        'w_uq': nrm((DEPTH, Q_LORA, N_HEADS, QK_NOPE + QK_ROPE), Q_LORA ** -0.5),
        'kv_norm_g': gain(KV_LORA),
        'w_uk': nrm((DEPTH, KV_LORA, N_HEADS, QK_NOPE), KV_LORA ** -0.5),
        'w_uv': nrm((DEPTH, KV_LORA, N_HEADS, V_DIM), KV_LORA ** -0.5),
        'ssm_out_g': gain(D_SSM),
        'att_out_g': gain(D_ATT),
        'w_out': nrm((DEPTH, D_MIX, D_MODEL), D_MIX ** -0.5),
        'mix_post_g': gain(D_MODEL),
        'ffn2_pre_g': gain(D_MODEL),
        'ffn2_w_gate': nrm((DEPTH, D_MODEL, D_FF), D_MODEL ** -0.5),
        'ffn2_w_up': nrm((DEPTH, D_MODEL, D_FF), D_MODEL ** -0.5),
        'ffn2_w_down': nrm((DEPTH, D_FF, D_MODEL), D_FF ** -0.5),
        'ffn2_post_g': gain(D_MODEL),
    }


def reference(x_prompt, x_sample, cache_ckv, cache_kpe, state_ssm_re, state_ssm_im, page_table,
              ffn1_pre_g, ffn1_w_gate, ffn1_w_up, ffn1_w_down, ffn1_post_g,
              mix_pre_g, w_in, ssm_lam_re, ssm_lam_im, ssm_log_step, ssm_b_re, ssm_b_im,
              ssm_c_re, ssm_c_im, ssm_d, ssm_w_glu, ssm_b_glu,
              q_norm_g, w_uq, kv_norm_g, w_uk, w_uv,
              ssm_out_g, att_out_g, w_out, mix_post_g,
              ffn2_pre_g, ffn2_w_gate, ffn2_w_up, ffn2_w_down, ffn2_post_g):

    def layer(l, x, pos, h0_re, h0_im, attend):
        bsz, t = x.shape[0], x.shape[1]
        x = macaron_ffn(x, ffn1_pre_g[l], ffn1_w_gate[l], ffn1_w_up[l], ffn1_w_down[l], ffn1_post_g[l])
        h = rmsnorm(x, mix_pre_g[l])
        z = h @ w_in[l]
        u = z[..., OFF_U:OFF_Q]
        hq = z[..., OFF_Q:OFF_KV]
        hkv = z[..., OFF_KV:OFF_PE]
        hpe = z[..., OFF_PE:D_IN]
        y_ssm, h_re, h_im = s5_mixer(u, h0_re, h0_im, ssm_lam_re[l], ssm_lam_im[l], ssm_log_step[l],
                                     ssm_b_re[l], ssm_b_im[l], ssm_c_re[l], ssm_c_im[l],
                                     ssm_d[l], ssm_w_glu[l], ssm_b_glu[l])
        cos, sin = rope_angles(pos)
        q = jnp.einsum('btr,rhe->bthe', rmsnorm(hq, q_norm_g[l]), w_uq[l])
        q_pe = apply_rope(q[..., QK_NOPE:], cos, sin)
        q_lat = jnp.einsum('bthn,rhn->bthr', q[..., :QK_NOPE], w_uk[l])
        c_kv = rmsnorm(hkv, kv_norm_g[l])
        k_pe = apply_rope(hpe, cos, sin)
        o_lat = attend(q_lat, q_pe, c_kv, k_pe)
        y_att = jnp.einsum('bthr,rhv->bthv', o_lat, w_uv[l]).reshape(bsz, t, D_ATT)
        mixed = jnp.concatenate([rmsnorm(y_ssm, ssm_out_g[l]), rmsnorm(y_att, att_out_g[l])], axis=-1) @ w_out[l]
        x = x + rmsnorm(mixed, mix_post_g[l])
        x = macaron_ffn(x, ffn2_pre_g[l], ffn2_w_gate[l], ffn2_w_up[l], ffn2_w_down[l], ffn2_post_g[l])
        return x, c_kv, k_pe, h_re, h_im

    pos_prompt = jnp.arange(x_prompt.shape[1], dtype=jnp.float32)
    pos_sample = PAST_LEN + jnp.arange(x_sample.shape[1], dtype=jnp.float32)
    zero_state = jnp.zeros((x_prompt.shape[0], N_SSM_GROUPS, SSM_STATE), jnp.float32)

    y_p = x_prompt
    y_s = x_sample
    ckv_p, kpe_p, re_p, im_p = [], [], [], []
    ckv_s, kpe_s, re_s, im_s = [], [], [], []
    for l in range(DEPTH):
        y_p, c1, k1, r1, i1 = layer(l, y_p, pos_prompt, zero_state, zero_state, mla_prompt_attend)
        attend_s = functools.partial(mla_sample_attend, cache_ckv=cache_ckv, cache_kpe=cache_kpe,
                                     page_table=page_table, layer=l)
        y_s, c2, k2, r2, i2 = layer(l, y_s, pos_sample, state_ssm_re[l], state_ssm_im[l], attend_s)
        ckv_p.append(c1)
        kpe_p.append(k1)
        re_p.append(r1)
        im_p.append(i1)
        ckv_s.append(c2)
        kpe_s.append(k2)
        re_s.append(r2)
        im_s.append(i2)

    return (y_p, y_s,
            jnp.stack(ckv_p), jnp.stack(kpe_p), jnp.stack(re_p), jnp.stack(im_p),
            jnp.stack(ckv_s), jnp.stack(kpe_s), jnp.stack(re_s), jnp.stack(im_s))
```

```python
import functools
import math

import jax
import jax.numpy as jnp
from jax import lax
from jax.experimental import pallas as pl
from jax.experimental.pallas import tpu as pltpu

F32 = jnp.float32
BF16 = jnp.bfloat16

D_MODEL = 1024
D_FF = 2816
EPS = 1e-6
D_SSM = 512
SSM_GROUP = 16
N_GROUPS = D_SSM // SSM_GROUP
SSM_STATE = 64
GP = N_GROUPS * SSM_STATE
N_HEADS = 8
QK_NOPE = 64
QK_ROPE = 32
HALF_ROPE = QK_ROPE // 2
V_DIM = 64
D_ATT = N_HEADS * V_DIM
Q_LORA = 256
KV_LORA = 256
ROPE_THETA = 10000.0
SOFTMAX_SCALE = 1.0 / math.sqrt(QK_NOPE + QK_ROPE)
D_IN_PACKED = D_SSM + Q_LORA + KV_LORA + 2 * N_HEADS * HALF_ROPE
D_Q_PACKED = N_HEADS * QK_NOPE + 2 * N_HEADS * HALF_ROPE
PE_W = 2 * N_HEADS * HALF_ROPE

V7X_SUBLANES = 8
V7X_LANES = 128
VMEM_LIMIT = 56 * 1024 * 1024

FFN_TM = 512
MIX_TT = 64
ATT_TQ = 256
DEC_PAGES = 32
SCAN_LANES = 512


def _const_spec(shape):
    nd = len(shape)
    return pl.BlockSpec(shape, lambda *_: (0,) * nd, pipeline_mode=pl.Buffered(1))


def _rms(x, g):
    return x * lax.rsqrt(jnp.mean(x * x, axis=-1, keepdims=True) + EPS) * g


def _dot(a, b):
    return jnp.dot(a, b, preferred_element_type=F32)


def _dot_nt(a, b):
    return lax.dot_general(a, b, (((1,), (1,)), ((), ())), preferred_element_type=F32)


def _ffn(x, pre_g, wg_ref, wu_ref, wd_ref, post_g):
    h = _rms(x, pre_g).astype(BF16)
    gate = _dot(h, wg_ref[...])
    up = _dot(h, wu_ref[...])
    act = (gate * jax.nn.sigmoid(gate) * up).astype(BF16)
    f = _dot(act, wd_ref[...])
    return x + 0.5 * _rms(f, post_g)


def _ffn1_kernel(x_ref, pre_g, wg, wu, wd, post_g, o_ref):
    o_ref[...] = _ffn(x_ref[...], pre_g[...], wg, wu, wd, post_g[...])


def _ffn_weight_specs():
    return [_const_spec((1, D_MODEL)), _const_spec((D_MODEL, D_FF)), _const_spec((D_MODEL, D_FF)),
            _const_spec((D_FF, D_MODEL)), _const_spec((1, D_MODEL))]


def _ffn1_prompt(x, w):
    bsz, t, _ = x.shape
    tm = min(FFN_TM, t)
    return pl.pallas_call(
        _ffn1_kernel,
        grid=(bsz, t // tm),
        in_specs=[pl.BlockSpec((None, tm, D_MODEL), lambda b, i: (b, i, 0))] + _ffn_weight_specs(),
        out_specs=pl.BlockSpec((tm, D_MODEL), lambda b, i: (i, b)),
        out_shape=jax.ShapeDtypeStruct((t, bsz * D_MODEL), F32),
        compiler_params=pltpu.CompilerParams(dimension_semantics=("arbitrary", "arbitrary"),
                                             vmem_limit_bytes=VMEM_LIMIT),
        name="ffn1_prompt",
    )(x, *w)


def _ffn1_rows(x, w):
    r = x.shape[0]
    tm = min(FFN_TM, r)
    return pl.pallas_call(
        _ffn1_kernel,
        grid=(r // tm,),
        in_specs=[pl.BlockSpec((tm, D_MODEL), lambda i: (i, 0))] + _ffn_weight_specs(),
        out_specs=pl.BlockSpec((tm, D_MODEL), lambda i: (i, 0)),
        out_shape=jax.ShapeDtypeStruct((r, D_MODEL), F32),
        compiler_params=pltpu.CompilerParams(dimension_semantics=("arbitrary",),
                                             vmem_limit_bytes=VMEM_LIMIT),
        name="ffn1_rows",
    )(x, *w)


def _ffn2_kernel(x_ref, ys_ref, ya_ref, wo_s, wo_a, mix_g, pre_g, wg, wu, wd, post_g, o_ref):
    mixed = _dot(ys_ref[...], wo_s[...]) + _dot(ya_ref[...], wo_a[...])
    x = x_ref[...] + _rms(mixed, mix_g[...])
    o_ref[...] = _ffn(x, pre_g[...], wg, wu, wd, post_g[...])


def _ffn2_weight_specs():
    return [_const_spec((D_SSM, D_MODEL)), _const_spec((D_ATT, D_MODEL)), _const_spec((1, D_MODEL))] \
        + _ffn_weight_specs()


def _ffn2_prompt(x1, ys, ya, w, bsz):
    t = x1.shape[0]
    tm = min(FFN_TM, t)
    return pl.pallas_call(
        _ffn2_kernel,
        grid=(bsz, t // tm),
        in_specs=[pl.BlockSpec((tm, D_MODEL), lambda b, i: (i, b)),
                  pl.BlockSpec((tm, D_SSM), lambda b, i: (i, b)),
                  pl.BlockSpec((tm, D_ATT), lambda b, i: (i, b))] + _ffn2_weight_specs(),
        out_specs=pl.BlockSpec((None, tm, D_MODEL), lambda b, i: (b, i, 0)),
        out_shape=jax.ShapeDtypeStruct((bsz, t, D_MODEL), F32),
        compiler_params=pltpu.CompilerParams(dimension_semantics=("arbitrary", "arbitrary"),
                                             vmem_limit_bytes=VMEM_LIMIT),
        name="ffn2_prompt",
    )(x1, ys, ya, *w)


def _ffn2_rows(x1, ys, ya, w):
    r = x1.shape[0]
    tm = min(FFN_TM, r)
    return pl.pallas_call(
        _ffn2_kernel,
        grid=(r // tm,),
        in_specs=[pl.BlockSpec((tm, D_MODEL), lambda i: (i, 0)),
                  pl.BlockSpec((tm, D_SSM), lambda i: (i, 0)),
                  pl.BlockSpec((tm, D_ATT), lambda i: (i, 0))] + _ffn2_weight_specs(),
        out_specs=pl.BlockSpec((tm, D_MODEL), lambda i: (i, 0)),
        out_shape=jax.ShapeDtypeStruct((r, D_MODEL), F32),
        compiler_params=pltpu.CompilerParams(dimension_semantics=("arbitrary",),
                                             vmem_limit_bytes=VMEM_LIMIT),
        name="ffn2_rows",
    )(x1, ys, ya, *w)


def _s5_prep_kernel(lr_ref, li_ref, ls_ref, br_ref, bi_ref, cr_ref, ci_ref,
                    are_ref, aim_ref, bb_ref, cc_ref):
    lr = lr_ref[...]
    li = li_ref[...]
    dt = jnp.exp(ls_ref[...])
    mag = jnp.exp(lr * dt)
    a_re = mag * jnp.cos(li * dt)
    a_im = mag * jnp.sin(li * dt)
    den = lr * lr + li * li
    n_re = a_re - 1.0
    z_re = (n_re * lr + a_im * li) / den
    z_im = (a_im * lr - n_re * li) / den
    are_ref[...] = a_re
    aim_ref[...] = a_im

    row_g = lax.shift_right_logical(lax.broadcasted_iota(jnp.int32, (D_SSM, GP), 0), 4)
    col_g = lax.shift_right_logical(lax.broadcasted_iota(jnp.int32, (D_SSM, GP), 1), 6)
    same = row_g == col_g
    br = br_ref[...]
    bi = bi_ref[...]
    bb_ref[:, :GP] = jnp.where(same, z_re * br - z_im * bi, 0.0).astype(BF16)
    bb_ref[:, GP:] = jnp.where(same, z_re * bi + z_im * br, 0.0).astype(BF16)

    row_g = lax.shift_right_logical(lax.broadcasted_iota(jnp.int32, (GP, D_SSM), 0), 6)
    col_g = lax.shift_right_logical(lax.broadcasted_iota(jnp.int32, (GP, D_SSM), 1), 4)
    same = row_g == col_g
    cc_ref[:GP, :] = jnp.where(same, cr_ref[...], 0.0).astype(BF16)
    cc_ref[GP:, :] = jnp.where(same, -ci_ref[...], 0.0).astype(BF16)


def _s5_prep(lam_re, lam_im, log_step, b_re, b_im, c_re, c_im):
    lr = lam_re.reshape(1, GP)
    li = lam_im.reshape(1, GP)
    ls = jnp.repeat(log_step, SSM_STATE).reshape(1, GP)
    br = jnp.tile(b_re.transpose(2, 0, 1).reshape(SSM_GROUP, GP), (N_GROUPS, 1))
    bi = jnp.tile(b_im.transpose(2, 0, 1).reshape(SSM_GROUP, GP), (N_GROUPS, 1))
    cr = jnp.tile(c_re.transpose(0, 2, 1).reshape(GP, SSM_GROUP), (1, N_GROUPS))
    ci = jnp.tile(c_im.transpose(0, 2, 1).reshape(GP, SSM_GROUP), (1, N_GROUPS))
    return pl.pallas_call(
        _s5_prep_kernel,
        out_shape=(jax.ShapeDtypeStruct((1, GP), F32), jax.ShapeDtypeStruct((1, GP), F32),
                   jax.ShapeDtypeStruct((D_SSM, 2 * GP), BF16), jax.ShapeDtypeStruct((2 * GP, D_SSM), BF16)),
        compiler_params=pltpu.CompilerParams(vmem_limit_bytes=VMEM_LIMIT),
        name="s5_prep",
    )(lr, li, ls, br, bi, cr, ci)


def _mixer_kernel(nb, tt,
                  x_ref, cos_ref, sin_ref, h0_ref, pre_g, w_in, bb, are_ref, aim_ref, cc, d_skip,
                  w_glu, b_glu, ssm_g, q_g, w_uq, w_uk, kv_g,
                  ys_ref, qlat_ref, qpe_ref, kcat_ref, ckv_ref, kpe_ref, st_ref,
                  hbuf):
    @pl.when(pl.program_id(0) == 0)
    def _():
        st_ref[...] = h0_ref[...]

    x = x_ref[...]
    h = _rms(x, pre_g[...]).astype(BF16)
    z = _dot(h, w_in[...])
    u = z[:, :D_SSM]
    hq = z[:, D_SSM:D_SSM + Q_LORA]
    hkv = z[:, D_SSM + Q_LORA:D_SSM + Q_LORA + KV_LORA]
    hpe = z[:, D_SSM + Q_LORA + KV_LORA:]

    hbuf[...] = _dot(u.astype(BF16), bb[...])
    n_sub = nb // V7X_SUBLANES
    for lc in range(GP // SCAN_LANES):
        l_re = pl.ds(lc * SCAN_LANES, SCAN_LANES)
        l_im = pl.ds(GP + lc * SCAN_LANES, SCAN_LANES)
        a_re = are_ref[:, l_re]
        a_im = aim_ref[:, l_re]

        def sub_body(sg, carry, l_re=l_re, l_im=l_im, a_re=a_re, a_im=a_im):
            base = pl.multiple_of(sg * V7X_SUBLANES, V7X_SUBLANES)
            rows0 = pl.ds(base, V7X_SUBLANES)

            def t_body(t, state):
                s_re, s_im = state
                rows = pl.ds(pl.multiple_of(t * nb + base, V7X_SUBLANES), V7X_SUBLANES)
                n_re = a_re * s_re - a_im * s_im + hbuf[rows, l_re]
                n_im = a_re * s_im + a_im * s_re + hbuf[rows, l_im]
                hbuf[rows, l_re] = n_re
                hbuf[rows, l_im] = n_im
                return n_re, n_im

            s_re, s_im = lax.fori_loop(0, tt, t_body, (st_ref[rows0, l_re], st_ref[rows0, l_im]),
                                       unroll=min(tt, 4))
            st_ref[rows0, l_re] = s_re
            st_ref[rows0, l_im] = s_im
            return carry

        lax.fori_loop(0, n_sub, sub_body, 0)

    y = _dot(hbuf[...].astype(BF16), cc[...]) + d_skip[...] * u
    y = jax.nn.gelu(y, approximate=True)
    y = y * jax.nn.sigmoid(_dot(y.astype(BF16), w_glu[...]) + b_glu[...])
    ys_ref[...] = _rms(y, ssm_g[...]).astype(BF16)

    cos = cos_ref[...]
    sin = sin_ref[...]
    q = _dot(_rms(hq, q_g[...]).astype(BF16), w_uq[...]) * SOFTMAX_SCALE
    q_nope = q[:, :N_HEADS * QK_NOPE]
    q1 = q[:, N_HEADS * QK_NOPE:N_HEADS * QK_NOPE + V7X_LANES]
    q2 = q[:, N_HEADS * QK_NOPE + V7X_LANES:]
    qpe_ref[:, :V7X_LANES] = (q1 * cos - q2 * sin).astype(BF16)
    qpe_ref[:, V7X_LANES:] = (q2 * cos + q1 * sin).astype(BF16)
    qlat_ref[...] = _dot(q_nope.astype(BF16), w_uk[...]).astype(BF16)

    ckv = _rms(hkv, kv_g[...])
    ckv_ref[...] = ckv
    k1 = hpe[:, :V7X_LANES]
    k2 = hpe[:, V7X_LANES:]
    r1 = k1 * cos - k2 * sin
    r2 = k2 * cos + k1 * sin
    kcat_ref[:, :KV_LORA] = ckv.astype(BF16)
    kcat_ref[:, KV_LORA:KV_LORA + V7X_LANES] = r1.astype(BF16)
    kcat_ref[:, KV_LORA + V7X_LANES:] = r2.astype(BF16)
    kpe_ref[:, :HALF_ROPE] = r1[:, :HALF_ROPE]
    kpe_ref[:, HALF_ROPE:] = r2[:, :HALF_ROPE]


def _mixer(x1, cos, sin, h0, w, nb, tt):
    r_all = x1.shape[0]
    r = tt * nb
    row = lambda width: pl.BlockSpec((r, width), lambda i: (i, 0))
    weight_specs = [
        _const_spec((1, D_MODEL)), _const_spec((D_MODEL, D_IN_PACKED)), _const_spec((D_SSM, 2 * GP)),
        _const_spec((1, GP)), _const_spec((1, GP)), _const_spec((2 * GP, D_SSM)), _const_spec((1, D_SSM)),
        _const_spec((D_SSM, D_SSM)), _const_spec((1, D_SSM)), _const_spec((1, D_SSM)),
        _const_spec((1, Q_LORA)), _const_spec((Q_LORA, D_Q_PACKED)),
        _const_spec((N_HEADS * QK_NOPE, N_HEADS * KV_LORA)), _const_spec((1, KV_LORA)),
    ]
    return pl.pallas_call(
        functools.partial(_mixer_kernel, nb, tt),
        grid=(r_all // r,),
        in_specs=[row(D_MODEL), row(V7X_LANES), row(V7X_LANES), _const_spec((nb, 2 * GP))] + weight_specs,
        out_specs=[row(D_SSM), row(N_HEADS * KV_LORA), row(PE_W), row(KV_LORA + PE_W), row(KV_LORA),
                   row(QK_ROPE), pl.BlockSpec((nb, 2 * GP), lambda i: (0, 0))],
        out_shape=[jax.ShapeDtypeStruct((r_all, D_SSM), BF16),
                   jax.ShapeDtypeStruct((r_all, N_HEADS * KV_LORA), BF16),
                   jax.ShapeDtypeStruct((r_all, PE_W), BF16),
                   jax.ShapeDtypeStruct((r_all, KV_LORA + PE_W), BF16),
                   jax.ShapeDtypeStruct((r_all, KV_LORA), F32),
                   jax.ShapeDtypeStruct((r_all, QK_ROPE), F32),
                   jax.ShapeDtypeStruct((nb, 2 * GP), F32)],
        scratch_shapes=[pltpu.VMEM((r, 2 * GP), F32)],
        compiler_params=pltpu.CompilerParams(dimension_semantics=("arbitrary",),
                                             vmem_limit_bytes=VMEM_LIMIT),
        name="mixer",
    )(x1, cos, sin, h0, *w)


def _prompt_attn_kernel(tq, qlat_ref, qpe_ref, kcat_ref, w_uv, att_g, ya_ref, acc_ref):
    i = pl.program_id(1)
    qpe = qpe_ref[...]
    head_of_lane = lax.shift_right_logical(
        lax.broadcasted_iota(jnp.int32, (tq, PE_W), 1) & (V7X_LANES - 1), 4)
    q_rows = []
    for hd in range(N_HEADS):
        q_rows.append(jnp.concatenate(
            [qlat_ref[:, hd * KV_LORA:(hd + 1) * KV_LORA],
             jnp.where(head_of_lane == hd, qpe, jnp.zeros_like(qpe))], axis=-1))
    qs = jnp.concatenate(q_rows, axis=0)
    rows = N_HEADS * tq

    def block(j, m, l, masked):
        k = kcat_ref[pl.ds(pl.multiple_of(j * tq, tq), tq), :]
        s = _dot_nt(qs, k)
        if masked:
            q_pos = lax.broadcasted_iota(jnp.int32, (rows, tq), 0) & (tq - 1)
            k_pos = lax.broadcasted_iota(jnp.int32, (rows, tq), 1)
            s = jnp.where(k_pos <= q_pos, s, -jnp.inf)
        m_new = jnp.maximum(m, jnp.max(s, axis=-1, keepdims=True))
        alpha = jnp.exp(m - m_new)
        p = jnp.exp(s - m_new)
        l_new = alpha * l + jnp.sum(p, axis=-1, keepdims=True)
        acc_ref[...] = alpha * acc_ref[...] + _dot(p.astype(BF16), k[:, :KV_LORA])
        return m_new, l_new

    acc_ref[...] = jnp.zeros_like(acc_ref)
    m0 = jnp.full((rows, 1), -jnp.inf, F32)
    l0 = jnp.zeros((rows, 1), F32)
    m, l = block(i, m0, l0, True)
    m, l = lax.fori_loop(0, i, lambda j, c: block(j, c[0], c[1], False), (m, l))

    o = acc_ref[...] / l
    o_cat = jnp.concatenate([o[hd * tq:(hd + 1) * tq, :] for hd in range(N_HEADS)], axis=-1)
    y = _dot(o_cat.astype(BF16), w_uv[...])
    ya_ref[...] = _rms(y, att_g[...]).astype(BF16)


def _prompt_attn(qlat, qpe, kcat, w_uv, att_g, bsz):
    t = qlat.shape[0]
    tq = min(ATT_TQ, t)
    return pl.pallas_call(
        functools.partial(_prompt_attn_kernel, tq),
        grid=(bsz, t // tq),
        in_specs=[pl.BlockSpec((tq, N_HEADS * KV_LORA), lambda b, i: (i, b)),
                  pl.BlockSpec((tq, PE_W), lambda b, i: (i, b)),
                  pl.BlockSpec((t, KV_LORA + PE_W), lambda b, i: (0, b)),
                  _const_spec((N_HEADS * KV_LORA, D_ATT)), _const_spec((1, D_ATT))],
        out_specs=pl.BlockSpec((tq, D_ATT), lambda b, i: (i, b)),
        out_shape=jax.ShapeDtypeStruct((t, bsz * D_ATT), BF16),
        scratch_shapes=[pltpu.VMEM((N_HEADS * tq, KV_LORA), F32)],
        compiler_params=pltpu.CompilerParams(dimension_semantics=("arbitrary", "arbitrary"),
                                             vmem_limit_bytes=VMEM_LIMIT),
        name="prompt_attn",
    )(qlat, qpe, kcat, w_uv, att_g)


def _decode_attn_kernel(n_chunks, dec_t,
                        pt_ref, qlat_ref, qpe_ref, cnew_ref, penew_ref, ckv_hbm, kpe_hbm,
                        o_ref, kvbuf, pebuf, sem):
    b = pl.program_id(0)
    nb = pl.num_programs(0)
    rows = N_HEADS * dec_t

    def page_copies(bb, chunk, slot, p):
        page = pt_ref[bb, chunk * DEC_PAGES + p]
        return (pltpu.make_async_copy(ckv_hbm.at[page], kvbuf.at[slot, p], sem.at[0, slot]),
                pltpu.make_async_copy(kpe_hbm.at[page], pebuf.at[slot, p], sem.at[1, slot]))

    def start_chunk(bb, chunk, slot):
        for p in range(DEC_PAGES):
            for cp in page_copies(bb, chunk, slot, p):
                cp.start()

    def wait_chunk(bb, chunk, slot):
        for p in range(DEC_PAGES):
            for cp in page_copies(bb, chunk, slot, p):
                cp.wait()

    @pl.when(b == 0)
    def _():
        start_chunk(0, 0, 0)

    qlat = qlat_ref[...]
    qpe = qpe_ref[...]
    qlat_f = qlat.astype(F32)
    qpe_f = qpe.astype(F32)

    t_of_row = lax.broadcasted_iota(jnp.int32, (rows, 1), 0) & (dec_t - 1)
    s_new = []
    for j in range(dec_t):
        sj = (jnp.sum(qlat_f * cnew_ref[j:j + 1, :], axis=-1, keepdims=True)
              + jnp.sum(qpe_f * penew_ref[j:j + 1, :], axis=-1, keepdims=True))
        s_new.append(jnp.where(t_of_row >= j, sj, -jnp.inf))
    m = s_new[0]
    for j in range(1, dec_t):
        m = jnp.maximum(m, s_new[j])
    l = jnp.zeros((rows, 1), F32)
    acc = jnp.zeros((rows, KV_LORA), F32)
    for j in range(dec_t):
        pj = jnp.exp(s_new[j] - m)
        l = l + pj
        acc = acc + pj * cnew_ref[j:j + 1, :]

    for chunk in range(n_chunks):
        slot = chunk % 2
        if chunk + 1 < n_chunks:
            start_chunk(b, chunk + 1, 1 - slot)
        else:
            @pl.when(b + 1 < nb)
            def _():
                start_chunk(b + 1, 0, 1 - slot)
        wait_chunk(b, chunk, slot)
        kv = kvbuf[slot].reshape(DEC_PAGES * V7X_LANES, KV_LORA).astype(BF16)
        pe = pebuf[slot].reshape(DEC_PAGES * V7X_LANES, QK_ROPE).astype(BF16)
        s = _dot_nt(qlat, kv) + _dot_nt(qpe, pe)
        m_new = jnp.maximum(m, jnp.max(s, axis=-1, keepdims=True))
        alpha = jnp.exp(m - m_new)
        p = jnp.exp(s - m_new)
        l = alpha * l + jnp.sum(p, axis=-1, keepdims=True)
        acc = alpha * acc + _dot(p.astype(BF16), kv)
        m = m_new

    o_ref[...] = acc / l


def _decode_attn(page_table, qlat, qpe, cnew, penew, cache_ckv, cache_kpe):
    bsz, n_pages = page_table.shape
    dec_t = cnew.shape[1]
    page = cache_ckv.shape[1]
    assert page == V7X_LANES and n_pages % (2 * DEC_PAGES) == 0 and dec_t & (dec_t - 1) == 0
    rows = N_HEADS * dec_t
    n_chunks = n_pages // DEC_PAGES
    grid_spec = pltpu.PrefetchScalarGridSpec(
        num_scalar_prefetch=1,
        grid=(bsz,),
        in_specs=[pl.BlockSpec((None, rows, KV_LORA), lambda b, pt: (b, 0, 0)),
                  pl.BlockSpec((None, rows, QK_ROPE), lambda b, pt: (b, 0, 0)),
                  pl.BlockSpec((None, dec_t, KV_LORA), lambda b, pt: (b, 0, 0)),
                  pl.BlockSpec((None, dec_t, QK_ROPE), lambda b, pt: (b, 0, 0)),
                  pl.BlockSpec(memory_space=pl.ANY),
                  pl.BlockSpec(memory_space=pl.ANY)],
        out_specs=pl.BlockSpec((None, rows, KV_LORA), lambda b, pt: (b, 0, 0)),
        scratch_shapes=[pltpu.VMEM((2, DEC_PAGES, page, KV_LORA), F32),
                        pltpu.VMEM((2, DEC_PAGES, page, QK_ROPE), F32),
                        pltpu.SemaphoreType.DMA((2, 2))],
    )
    return pl.pallas_call(
        functools.partial(_decode_attn_kernel, n_chunks, dec_t),
        grid_spec=grid_spec,
        out_shape=jax.ShapeDtypeStruct((bsz, rows, KV_LORA), F32),
        compiler_params=pltpu.CompilerParams(dimension_semantics=("arbitrary",),
                                             vmem_limit_bytes=VMEM_LIMIT),
        name="decode_attn",
    )(page_table, qlat, qpe, cnew, penew, cache_ckv, cache_kpe)


def _uv_kernel(o_ref, w_uv, att_g, ya_ref):
    y = _dot(o_ref[...].astype(BF16), w_uv[...])
    ya_ref[...] = _rms(y, att_g[...]).astype(BF16)


def _uv_proj(o_cat, w_uv, att_g):
    r = o_cat.shape[0]
    return pl.pallas_call(
        _uv_kernel,
        out_shape=jax.ShapeDtypeStruct((r, D_ATT), BF16),
        compiler_params=pltpu.CompilerParams(vmem_limit_bytes=VMEM_LIMIT),
        name="uv_proj",
    )(o_cat, w_uv, att_g)


def _rope_tables(pos, nb):
    inv = ROPE_THETA ** (-jnp.arange(0, QK_ROPE, 2, dtype=F32) / QK_ROPE)
    ang = pos[:, None] * inv[None, :]
    cos = jnp.repeat(jnp.tile(jnp.cos(ang), (1, N_HEADS)), nb, axis=0)
    sin = jnp.repeat(jnp.tile(jnp.sin(ang), (1, N_HEADS)), nb, axis=0)
    return cos, sin


def kernel(x_prompt, x_sample, cache_ckv, cache_kpe, state_ssm_re, state_ssm_im, page_table, ffn1_pre_g, ffn1_w_gate, ffn1_w_up, ffn1_w_down, ffn1_post_g, mix_pre_g, w_in, ssm_lam_re, ssm_lam_im, ssm_log_step, ssm_b_re, ssm_b_im, ssm_c_re, ssm_c_im, ssm_d, ssm_w_glu, ssm_b_glu, q_norm_g, w_uq, kv_norm_g, w_uk, w_uv, ssm_out_g, att_out_g, w_out, mix_post_g, ffn2_pre_g, ffn2_w_gate, ffn2_w_up, ffn2_w_down, ffn2_post_g):
    assert ffn1_pre_g.shape[0] == 1, "single-layer trunk"
    bsz, seq, _ = x_prompt.shape
    dbs, dseq, _ = x_sample.shape
    past_len = page_table.shape[1] * cache_ckv.shape[2]
    assert bsz == V7X_SUBLANES and dbs % V7X_SUBLANES == 0

    row = lambda v: v[0].reshape(1, -1)
    ffn1_w = (row(ffn1_pre_g), ffn1_w_gate[0].astype(BF16), ffn1_w_up[0].astype(BF16),
              ffn1_w_down[0].astype(BF16), row(ffn1_post_g))
    ffn2_w = (w_out[0, :D_SSM].astype(BF16), w_out[0, D_SSM:].astype(BF16), row(mix_post_g),
              row(ffn2_pre_g), ffn2_w_gate[0].astype(BF16), ffn2_w_up[0].astype(BF16),
              ffn2_w_down[0].astype(BF16), row(ffn2_post_g))
    off_pe = D_SSM + Q_LORA + KV_LORA
    w_in_p = jnp.concatenate(
        [w_in[0, :, :off_pe],
         jnp.tile(w_in[0, :, off_pe:off_pe + HALF_ROPE], (1, N_HEADS)),
         jnp.tile(w_in[0, :, off_pe + HALF_ROPE:], (1, N_HEADS))], axis=1).astype(BF16)
    wq = w_uq[0]
    w_uq_p = jnp.concatenate(
        [wq[:, :, :QK_NOPE].reshape(Q_LORA, N_HEADS * QK_NOPE),
         wq[:, :, QK_NOPE:QK_NOPE + HALF_ROPE].reshape(Q_LORA, N_HEADS * HALF_ROPE),
         wq[:, :, QK_NOPE + HALF_ROPE:].reshape(Q_LORA, N_HEADS * HALF_ROPE)], axis=1).astype(BF16)
    eye_h = jnp.eye(N_HEADS, dtype=F32)
    w_uk_bd = jnp.einsum('hnr,hg->hngr', w_uk[0].transpose(1, 2, 0), eye_h).reshape(
        N_HEADS * QK_NOPE, N_HEADS * KV_LORA).astype(BF16)
    w_uv_bd = jnp.einsum('hrv,hg->hrgv', w_uv[0].transpose(1, 0, 2), eye_h).reshape(
        N_HEADS * KV_LORA, D_ATT).astype(BF16)
    a_re, a_im, bb, cc = _s5_prep(ssm_lam_re[0], ssm_lam_im[0], ssm_log_step[0],
                                  ssm_b_re[0], ssm_b_im[0], ssm_c_re[0], ssm_c_im[0])
    mix_w = (row(mix_pre_g), w_in_p, bb, a_re, a_im, cc, row(ssm_d), ssm_w_glu[0].astype(BF16),
             row(ssm_b_glu), row(ssm_out_g), row(q_norm_g), w_uq_p, w_uk_bd, row(kv_norm_g))
    att_g = row(att_out_g)

    cos_p, sin_p = _rope_tables(jnp.arange(seq, dtype=F32), bsz)
    x1_p = _ffn1_prompt(x_prompt, ffn1_w)
    ys_p, qlat_p, qpe_p, kcat_p, ckv_p, kpe_p, st_p = _mixer(
        x1_p.reshape(seq * bsz, D_MODEL), cos_p, sin_p, jnp.zeros((bsz, 2 * GP), F32), mix_w,
        bsz, min(MIX_TT, seq))
    ya_p = _prompt_attn(qlat_p.reshape(seq, -1), qpe_p.reshape(seq, -1), kcat_p.reshape(seq, -1),
                        w_uv_bd, att_g, bsz)
    y_p = _ffn2_prompt(x1_p, ys_p.reshape(seq, -1), ya_p, ffn2_w, bsz)

    cos_s, sin_s = _rope_tables(past_len + jnp.arange(dseq, dtype=F32), dbs)
    xs = x_sample.transpose(1, 0, 2).reshape(dseq * dbs, D_MODEL)
    x1_s = _ffn1_rows(xs, ffn1_w)
    h0_s = jnp.concatenate([state_ssm_re[0].reshape(dbs, GP), state_ssm_im[0].reshape(dbs, GP)], axis=1)
    ys_s, qlat_s, qpe_s, _, ckv_s, kpe_s, st_s = _mixer(x1_s, cos_s, sin_s, h0_s, mix_w, dbs, dseq)
    qlat_d = qlat_s.reshape(dseq, dbs, N_HEADS, KV_LORA).transpose(1, 2, 0, 3).reshape(
        dbs, N_HEADS * dseq, KV_LORA)
    qpe_d = qpe_s.reshape(dseq, dbs, 2, N_HEADS, HALF_ROPE).transpose(1, 3, 0, 2, 4).reshape(
        dbs, N_HEADS * dseq, QK_ROPE)
    cnew = ckv_s.reshape(dseq, dbs, KV_LORA).transpose(1, 0, 2)
    penew = kpe_s.reshape(dseq, dbs, QK_ROPE).transpose(1, 0, 2)
    o_d = _decode_attn(page_table, qlat_d, qpe_d, cnew, penew, cache_ckv[0], cache_kpe[0])
    o_cat = o_d.reshape(dbs, N_HEADS, dseq, KV_LORA).transpose(2, 0, 1, 3).reshape(
        dseq * dbs, N_HEADS * KV_LORA)
    ya_s = _uv_proj(o_cat, w_uv_bd, att_g)
    y_s = _ffn2_rows(x1_s, ys_s, ya_s, ffn2_w).reshape(dseq, dbs, D_MODEL).transpose(1, 0, 2)

    tm_to_bm = lambda v, n, t: v.reshape(t, n, -1).transpose(1, 0, 2)[None]
    state = lambda st, n: (st[:, :GP].reshape(1, n, N_GROUPS, SSM_STATE),
                           st[:, GP:].reshape(1, n, N_GROUPS, SSM_STATE))
    re_p, im_p = state(st_p, bsz)
    re_s, im_s = state(st_s, dbs)
    return (y_p, y_s,
            tm_to_bm(ckv_p, bsz, seq), tm_to_bm(kpe_p, bsz, seq), re_p, im_p,
            cnew[None], penew[None], re_s, im_s)
```

```python
import functools
import math

import jax
import jax.numpy as jnp
from jax import lax
from jax.experimental import pallas as pl
from jax.experimental.pallas import tpu as pltpu

F32 = jnp.float32
BF16 = jnp.bfloat16

D_MODEL = 1024
D_FF = 2816
EPS = 1e-6
D_SSM = 512
SSM_GROUP = 16
N_GROUPS = D_SSM // SSM_GROUP
SSM_STATE = 64
GP = N_GROUPS * SSM_STATE
N_HEADS = 8
QK_NOPE = 64
QK_ROPE = 32
HALF_ROPE = QK_ROPE // 2
V_DIM = 64
D_ATT = N_HEADS * V_DIM
Q_LORA = 256
KV_LORA = 256
ROPE_THETA = 10000.0
SOFTMAX_SCALE = 1.0 / math.sqrt(QK_NOPE + QK_ROPE)
D_IN_PACKED = D_SSM + Q_LORA + KV_LORA + 2 * N_HEADS * HALF_ROPE
D_Q_PACKED = N_HEADS * QK_NOPE + 2 * N_HEADS * HALF_ROPE
PE_W = 2 * N_HEADS * HALF_ROPE

V7X_SUBLANES = 8
V7X_LANES = 128
VMEM_LIMIT = 56 * 1024 * 1024

FFN_TM = 512
MIX_TT = 64
ATT_TQ = 256
DEC_REGION_PAGES = 64
DEC_SUB_PAGES = 16
SCAN_SLABS = 4
N_SLABS = GP // V7X_LANES


def _const_spec(shape):
    nd = len(shape)
    return pl.BlockSpec(shape, lambda *_: (0,) * nd, pipeline_mode=pl.Buffered(1))


def _rms(x, g):
    return x * lax.rsqrt(jnp.mean(x * x, axis=-1, keepdims=True) + EPS) * g


def _dot(a, b):
    return jnp.dot(a, b, preferred_element_type=F32)


def _dot_nt(a, b):
    return lax.dot_general(a, b, (((1,), (1,)), ((), ())), preferred_element_type=F32)


def _ffn(x, pre_g, wg_ref, wu_ref, wd_ref, post_g):
    h = _rms(x, pre_g).astype(BF16)
    gate = _dot(h, wg_ref[...])
    up = _dot(h, wu_ref[...])
    act = (gate * jax.nn.sigmoid(gate) * up).astype(BF16)
    f = _dot(act, wd_ref[...])
    return x + 0.5 * _rms(f, post_g)


def _params(n_grid_axes):
    return pltpu.CompilerParams(dimension_semantics=("arbitrary",) * n_grid_axes,
                                vmem_limit_bytes=VMEM_LIMIT)


def _ffn1_kernel(x_ref, pre_g, wg, wu, wd, post_g, o_ref):
    o_ref[...] = _ffn(x_ref[...], pre_g[...], wg, wu, wd, post_g[...])


def _ffn_weight_specs():
    return [_const_spec((1, D_MODEL)), _const_spec((D_MODEL, D_FF)), _const_spec((D_MODEL, D_FF)),
            _const_spec((D_FF, D_MODEL)), _const_spec((1, D_MODEL))]


def _ffn1(x, w):
    r = x.shape[0]
    tm = min(FFN_TM, r)
    rows = pl.BlockSpec((tm, D_MODEL), lambda i: (i, 0))
    return pl.pallas_call(
        _ffn1_kernel,
        grid=(r // tm,),
        in_specs=[rows] + _ffn_weight_specs(),
        out_specs=rows,
        out_shape=jax.ShapeDtypeStruct((r, D_MODEL), F32),
        compiler_params=_params(1),
        name="ffn1",
    )(x, *w)


def _ffn2_kernel(x_ref, ys_ref, ya_ref, wo_s, wo_a, mix_g, pre_g, wg, wu, wd, post_g, o_ref):
    mixed = _dot(ys_ref[...], wo_s[...]) + _dot(ya_ref[...], wo_a[...])
    x = x_ref[...] + _rms(mixed, mix_g[...])
    o_ref[...] = _ffn(x, pre_g[...], wg, wu, wd, post_g[...])


def _ffn2(x1, ys, ya, w):
    r = x1.shape[0]
    tm = min(FFN_TM, r)
    rows = lambda width: pl.BlockSpec((tm, width), lambda i: (i, 0))
    weight_specs = [_const_spec((D_SSM, D_MODEL)), _const_spec((D_ATT, D_MODEL)),
                    _const_spec((1, D_MODEL))] + _ffn_weight_specs()
    return pl.pallas_call(
        _ffn2_kernel,
        grid=(r // tm,),
        in_specs=[rows(D_MODEL), rows(D_SSM), rows(D_ATT)] + weight_specs,
        out_specs=rows(D_MODEL),
        out_shape=jax.ShapeDtypeStruct((r, D_MODEL), F32),
        compiler_params=_params(1),
        name="ffn2",
    )(x1, ys, ya, *w)


def _s5_prep_kernel(lr_ref, li_ref, ls_ref, br_ref, bi_ref, cr_ref, ci_ref,
                    are_ref, aim_ref, bb_ref, cc_ref):
    lr = lr_ref[...]
    li = li_ref[...]
    dt = jnp.exp(ls_ref[...])
    mag = jnp.exp(lr * dt)
    a_re = mag * jnp.cos(li * dt)
    a_im = mag * jnp.sin(li * dt)
    den = lr * lr + li * li
    n_re = a_re - 1.0
    z_re = (n_re * lr + a_im * li) / den
    z_im = (a_im * lr - n_re * li) / den
    are_ref[...] = a_re
    aim_ref[...] = a_im

    row_g = lax.shift_right_logical(lax.broadcasted_iota(jnp.int32, (D_SSM, GP), 0), 4)
    col_g = lax.shift_right_logical(lax.broadcasted_iota(jnp.int32, (D_SSM, GP), 1), 6)
    same = row_g == col_g
    br = br_ref[...]
    bi = bi_ref[...]
    bb_ref[:, :GP] = jnp.where(same, z_re * br - z_im * bi, 0.0).astype(BF16)
    bb_ref[:, GP:] = jnp.where(same, z_re * bi + z_im * br, 0.0).astype(BF16)

    row_g = lax.shift_right_logical(lax.broadcasted_iota(jnp.int32, (GP, D_SSM), 0), 6)
    col_g = lax.shift_right_logical(lax.broadcasted_iota(jnp.int32, (GP, D_SSM), 1), 4)
    same = row_g == col_g
    cc_ref[:GP, :] = jnp.where(same, cr_ref[...], 0.0).astype(BF16)
    cc_ref[GP:, :] = jnp.where(same, -ci_ref[...], 0.0).astype(BF16)


def _s5_prep(lam_re, lam_im, log_step, b_re, b_im, c_re, c_im):
    lr = lam_re.reshape(1, GP)
    li = lam_im.reshape(1, GP)
    ls = jnp.repeat(log_step, SSM_STATE).reshape(1, GP)
    br = jnp.tile(b_re.transpose(2, 0, 1).reshape(SSM_GROUP, GP), (N_GROUPS, 1))
    bi = jnp.tile(b_im.transpose(2, 0, 1).reshape(SSM_GROUP, GP), (N_GROUPS, 1))
    cr = jnp.tile(c_re.transpose(0, 2, 1).reshape(GP, SSM_GROUP), (1, N_GROUPS))
    ci = jnp.tile(c_im.transpose(0, 2, 1).reshape(GP, SSM_GROUP), (1, N_GROUPS))
    return pl.pallas_call(
        _s5_prep_kernel,
        out_shape=(jax.ShapeDtypeStruct((1, GP), F32), jax.ShapeDtypeStruct((1, GP), F32),
                   jax.ShapeDtypeStruct((D_SSM, 2 * GP), BF16), jax.ShapeDtypeStruct((2 * GP, D_SSM), BF16)),
        compiler_params=pltpu.CompilerParams(vmem_limit_bytes=VMEM_LIMIT),
        name="s5_prep",
    )(lr, li, ls, br, bi, cr, ci)


def _scan_pitch(tt):
    return tt if tt % 16 else tt + V7X_SUBLANES


def _mixer_kernel(nb, tt,
                  x_ref, cos_ref, sin_ref, h0_ref, pre_g, w_in, bb, are_ref, aim_ref, cc, d_skip,
                  w_glu, b_glu, ssm_g, q_g, w_uq, w_uk, kv_g,
                  ys_ref, qlat_ref, qpe_ref, kcat_ref, ckv_ref, kpe_ref, st_ref,
                  hbuf):
    g, rpg, _ = x_ref.shape
    r = nb * tt
    pitch = _scan_pitch(tt)

    @pl.when(pl.program_id(0) == 0)
    def _():
        st_ref[...] = h0_ref[...]

    x = x_ref[...].reshape(r, D_MODEL)
    h = _rms(x, pre_g[...]).astype(BF16)
    z = _dot(h, w_in[...])
    u = z[:, :D_SSM]
    hq = z[:, D_SSM:D_SSM + Q_LORA]
    hkv = z[:, D_SSM + Q_LORA:D_SSM + Q_LORA + KV_LORA]
    hpe = z[:, D_SSM + Q_LORA + KV_LORA:]

    bu = _dot(u.astype(BF16), bb[...])
    for c in range(2 * N_SLABS):
        lanes = slice(c * V7X_LANES, (c + 1) * V7X_LANES)
        if pitch == tt:
            hbuf[c] = bu[:, lanes]
        else:
            for b in range(nb):
                hbuf[c, b * pitch:b * pitch + tt, :] = bu[b * tt:(b + 1) * tt, lanes]

    n_sub = nb // V7X_SUBLANES
    for lc in range(N_SLABS // SCAN_SLABS):
        slabs = [lc * SCAN_SLABS + k for k in range(SCAN_SLABS)]
        a_re = [are_ref[:, s * V7X_LANES:(s + 1) * V7X_LANES] for s in slabs]
        a_im = [aim_ref[:, s * V7X_LANES:(s + 1) * V7X_LANES] for s in slabs]

        def sub_body(sg, carry, slabs=slabs, a_re=a_re, a_im=a_im):
            seqs = pl.ds(pl.multiple_of(sg * V7X_SUBLANES, V7X_SUBLANES), V7X_SUBLANES)
            row0 = sg * (V7X_SUBLANES * pitch)

            def t_body(t, state):
                rows = pl.ds(row0 + t, V7X_SUBLANES, stride=pitch)
                new = []
                for k, s in enumerate(slabs):
                    s_re, s_im = state[2 * k], state[2 * k + 1]
                    n_re = a_re[k] * s_re - a_im[k] * s_im + hbuf[s, rows, :]
                    n_im = a_re[k] * s_im + a_im[k] * s_re + hbuf[N_SLABS + s, rows, :]
                    hbuf[s, rows, :] = n_re
                    hbuf[N_SLABS + s, rows, :] = n_im
                    new += [n_re, n_im]
                return tuple(new)

            init = []
            for s in slabs:
                init += [st_ref[seqs, s * V7X_LANES:(s + 1) * V7X_LANES],
                         st_ref[seqs, GP + s * V7X_LANES:GP + (s + 1) * V7X_LANES]]
            fin = lax.fori_loop(0, tt, t_body, tuple(init), unroll=min(tt, 4))
            for k, s in enumerate(slabs):
                st_ref[seqs, s * V7X_LANES:(s + 1) * V7X_LANES] = fin[2 * k]
                st_ref[seqs, GP + s * V7X_LANES:GP + (s + 1) * V7X_LANES] = fin[2 * k + 1]
            return carry

        lax.fori_loop(0, n_sub, sub_body, 0)

    def slab_rows(c):
        if pitch == tt:
            return hbuf[c]
        return jnp.concatenate([hbuf[c, b * pitch:b * pitch + tt, :] for b in range(nb)], axis=0)

    hs = jnp.concatenate([slab_rows(c) for c in range(2 * N_SLABS)], axis=1).astype(BF16)
    y = _dot(hs, cc[...]) + d_skip[...] * u
    y = jax.nn.gelu(y, approximate=True)
    y = y * jax.nn.sigmoid(_dot(y.astype(BF16), w_glu[...]) + b_glu[...])
    ys_ref[...] = _rms(y, ssm_g[...]).astype(BF16).reshape(g, rpg, D_SSM)

    cos = cos_ref[...]
    sin = sin_ref[...]

    def rope(x1, x2):
        x1 = x1.reshape(g, rpg, V7X_LANES)
        x2 = x2.reshape(g, rpg, V7X_LANES)
        return x1 * cos - x2 * sin, x2 * cos + x1 * sin

    q = _dot(_rms(hq, q_g[...]).astype(BF16), w_uq[...]) * SOFTMAX_SCALE
    q_nope = q[:, :N_HEADS * QK_NOPE]
    q1, q2 = rope(q[:, N_HEADS * QK_NOPE:N_HEADS * QK_NOPE + V7X_LANES],
                  q[:, N_HEADS * QK_NOPE + V7X_LANES:])
    qpe_ref[:, :, :V7X_LANES] = q1.astype(BF16)
    qpe_ref[:, :, V7X_LANES:] = q2.astype(BF16)
    qlat_ref[...] = _dot(q_nope.astype(BF16), w_uk[...]).astype(BF16).reshape(g, rpg, N_HEADS * KV_LORA)

    ckv = _rms(hkv, kv_g[...]).reshape(g, rpg, KV_LORA)
    ckv_ref[...] = ckv
    k1, k2 = rope(hpe[:, :V7X_LANES], hpe[:, V7X_LANES:])
    kcat_ref[:, :, :KV_LORA] = ckv.astype(BF16)
    kcat_ref[:, :, KV_LORA:KV_LORA + V7X_LANES] = k1.astype(BF16)
    kcat_ref[:, :, KV_LORA + V7X_LANES:] = k2.astype(BF16)
    kpe_ref[:, :, :HALF_ROPE] = k1[:, :, :HALF_ROPE]
    kpe_ref[:, :, HALF_ROPE:] = k2[:, :, :HALF_ROPE]


def _mixer(x1, cos, sin, h0, w, nb, tt):
    n_g, length, _ = x1.shape
    rpg = nb * tt // n_g
    assert length % rpg == 0 and rpg % 16 == 0 and nb % V7X_SUBLANES == 0
    pitch = _scan_pitch(tt)
    row = lambda width: pl.BlockSpec((n_g, rpg, width), lambda i: (0, i, 0))
    out = lambda width, dtype: jax.ShapeDtypeStruct((n_g, length, width), dtype)
    weight_specs = [
        _const_spec((1, D_MODEL)), _const_spec((D_MODEL, D_IN_PACKED)), _const_spec((D_SSM, 2 * GP)),
        _const_spec((1, GP)), _const_spec((1, GP)), _const_spec((2 * GP, D_SSM)), _const_spec((1, D_SSM)),
        _const_spec((D_SSM, D_SSM)), _const_spec((1, D_SSM)), _const_spec((1, D_SSM)),
        _const_spec((1, Q_LORA)), _const_spec((Q_LORA, D_Q_PACKED)),
        _const_spec((N_HEADS * QK_NOPE, N_HEADS * KV_LORA)), _const_spec((1, KV_LORA)),
    ]
    rope_spec = pl.BlockSpec((rpg, V7X_LANES), lambda i: (i, 0))
    return pl.pallas_call(
        functools.partial(_mixer_kernel, nb, tt),
        grid=(length // rpg,),
        in_specs=[row(D_MODEL), rope_spec, rope_spec, _const_spec((nb, 2 * GP))] + weight_specs,
        out_specs=[row(D_SSM), row(N_HEADS * KV_LORA), row(PE_W), row(KV_LORA + PE_W), row(KV_LORA),
                   row(QK_ROPE), pl.BlockSpec((nb, 2 * GP), lambda i: (0, 0))],
        out_shape=[out(D_SSM, BF16), out(N_HEADS * KV_LORA, BF16), out(PE_W, BF16),
                   out(KV_LORA + PE_W, BF16), out(KV_LORA, F32), out(QK_ROPE, F32),
                   jax.ShapeDtypeStruct((nb, 2 * GP), F32)],
        scratch_shapes=[pltpu.VMEM((2 * N_SLABS, nb * pitch, V7X_LANES), F32)],
        compiler_params=_params(1),
        name="mixer",
    )(x1, cos, sin, h0, *w)


def _prompt_attn_kernel(tq, qlat_ref, qpe_ref, kcat_ref, w_uv, att_g, ya_ref, acc_ref):
    i = pl.program_id(1)
    qpe = qpe_ref[...]
    head_of_lane = lax.shift_right_logical(
        lax.broadcasted_iota(jnp.int32, (tq, PE_W), 1) & (V7X_LANES - 1), 4)
    q_rows = []
    for hd in range(N_HEADS):
        q_rows.append(jnp.concatenate(
            [qlat_ref[:, hd * KV_LORA:(hd + 1) * KV_LORA],
             jnp.where(head_of_lane == hd, qpe, jnp.zeros_like(qpe))], axis=-1))
    qs = jnp.concatenate(q_rows, axis=0)
    rows = N_HEADS * tq

    def block(j, m, l, masked):
        k = kcat_ref[pl.ds(pl.multiple_of(j * tq, tq), tq), :]
        s = _dot_nt(qs, k)
        if masked:
            q_pos = lax.broadcasted_iota(jnp.int32, (rows, tq), 0) & (tq - 1)
            k_pos = lax.broadcasted_iota(jnp.int32, (rows, tq), 1)
            s = jnp.where(k_pos <= q_pos, s, -jnp.inf)
        m_new = jnp.maximum(m, jnp.max(s, axis=-1, keepdims=True))
        alpha = jnp.exp(m - m_new)
        p = jnp.exp(s - m_new)
        l_new = alpha * l + jnp.sum(p, axis=-1, keepdims=True)
        acc_ref[...] = alpha * acc_ref[...] + _dot(p.astype(BF16), k[:, :KV_LORA])
        return m_new, l_new

    acc_ref[...] = jnp.zeros_like(acc_ref)
    m0 = jnp.full((rows, 1), -jnp.inf, F32)
    l0 = jnp.zeros((rows, 1), F32)
    m, l = block(i, m0, l0, True)
    m, l = lax.fori_loop(0, i, lambda j, c: block(j, c[0], c[1], False), (m, l))

    o = acc_ref[...] / l
    o_cat = jnp.concatenate([o[hd * tq:(hd + 1) * tq, :] for hd in range(N_HEADS)], axis=-1)
    y = _dot(o_cat.astype(BF16), w_uv[...])
    ya_ref[...] = _rms(y, att_g[...]).astype(BF16)


def _prompt_attn(qlat, qpe, kcat, w_uv, att_g):
    bsz, t, _ = qlat.shape
    tq = min(ATT_TQ, t)
    assert t % tq == 0 and tq & (tq - 1) == 0
    return pl.pallas_call(
        functools.partial(_prompt_attn_kernel, tq),
        grid=(bsz, t // tq),
        in_specs=[pl.BlockSpec((None, tq, N_HEADS * KV_LORA), lambda b, i: (b, i, 0)),
                  pl.BlockSpec((None, tq, PE_W), lambda b, i: (b, i, 0)),
                  pl.BlockSpec((None, t, KV_LORA + PE_W), lambda b, i: (b, 0, 0)),
                  _const_spec((N_HEADS * KV_LORA, D_ATT)), _const_spec((1, D_ATT))],
        out_specs=pl.BlockSpec((None, tq, D_ATT), lambda b, i: (b, i, 0)),
        out_shape=jax.ShapeDtypeStruct((bsz, t, D_ATT), BF16),
        scratch_shapes=[pltpu.VMEM((N_HEADS * tq, KV_LORA), F32)],
        compiler_params=_params(2),
        name="prompt_attn",
    )(qlat, qpe, kcat, w_uv, att_g)


def _merge(m, l, acc, m_c, l_c, acc_c):
    m_new = jnp.maximum(m, m_c)
    w = jnp.exp(m - m_new)
    w_c = jnp.exp(m_c - m_new)
    return m_new, w * l + w_c * l_c, w * acc + w_c * acc_c


def _decode_attn_kernel(n_regions, dec_t,
                        pt_ref, qlat_ref, qpe_ref, cnew_ref, penew_ref, ckv_hbm, kpe_hbm,
                        o_ref, kvbuf, pebuf, sem):
    b = pl.program_id(0)
    nb = pl.num_programs(0)
    rows = N_HEADS * dec_t

    def page_copies(bb, region, slot, p):
        page = pt_ref[bb, region * DEC_REGION_PAGES + p]
        return (pltpu.make_async_copy(ckv_hbm.at[0, page], kvbuf.at[slot, p], sem.at[0, slot]),
                pltpu.make_async_copy(kpe_hbm.at[0, page], pebuf.at[slot, p], sem.at[1, slot]))

    def start_region(bb, region, slot):
        for p in range(DEC_REGION_PAGES):
            for cp in page_copies(bb, region, slot, p):
                cp.start()

    def wait_region(bb, region, slot):
        for p in range(DEC_REGION_PAGES):
            for cp in page_copies(bb, region, slot, p):
                cp.wait()

    @pl.when(b == 0)
    def _():
        start_region(0, 0, 0)

    qlat = qlat_ref[...]
    qpe = qpe_ref[...]
    qlat_f = qlat.astype(F32)
    qpe_f = qpe.astype(F32)

    t_of_row = lax.broadcasted_iota(jnp.int32, (rows, 1), 0) & (dec_t - 1)
    s_new = []
    for j in range(dec_t):
        sj = (jnp.sum(qlat_f * cnew_ref[j:j + 1, :], axis=-1, keepdims=True)
              + jnp.sum(qpe_f * penew_ref[j:j + 1, :], axis=-1, keepdims=True))
        s_new.append(jnp.where(t_of_row >= j, sj, -jnp.inf))
    m = s_new[0]
    for j in range(1, dec_t):
        m = jnp.maximum(m, s_new[j])
    l = jnp.zeros((rows, 1), F32)
    acc = jnp.zeros((rows, KV_LORA), F32)
    for j in range(dec_t):
        pj = jnp.exp(s_new[j] - m)
        l = l + pj
        acc = acc + pj * cnew_ref[j:j + 1, :]

    sub_keys = DEC_SUB_PAGES * V7X_LANES
    for region in range(n_regions):
        slot = region % 2
        if region + 1 < n_regions:
            start_region(b, region + 1, 1 - slot)
        else:
            @pl.when(b + 1 < nb)
            def _():
                start_region(b + 1, 0, 1 - slot)
        wait_region(b, region, slot)
        for sub in range(DEC_REGION_PAGES // DEC_SUB_PAGES):
            p0 = sub * DEC_SUB_PAGES
            kv = kvbuf[slot, p0:p0 + DEC_SUB_PAGES].reshape(sub_keys, KV_LORA).astype(BF16)
            pe_t = jnp.concatenate([pebuf[slot, p0 + p] for p in range(DEC_SUB_PAGES)],
                                   axis=1).astype(BF16)
            s = _dot_nt(qlat, kv) + _dot(qpe, pe_t)
            m_c = jnp.max(s, axis=-1, keepdims=True)
            p = jnp.exp(s - m_c)
            l_c = jnp.sum(p, axis=-1, keepdims=True)
            acc_c = _dot(p.astype(BF16), kv)
            m, l, acc = _merge(m, l, acc, m_c, l_c, acc_c)

    o_ref[...] = acc / l


def _decode_attn(page_table, qlat, qpe, cnew, penew, cache_ckv, cache_kpe_t):
    bsz, n_pages = page_table.shape
    dec_t = cnew.shape[1]
    page = cache_ckv.shape[2]
    assert page == V7X_LANES and n_pages % (2 * DEC_REGION_PAGES) == 0 and dec_t & (dec_t - 1) == 0
    rows = N_HEADS * dec_t
    n_regions = n_pages // DEC_REGION_PAGES
    per_seq = lambda r, width: pl.BlockSpec((None, r, width), lambda b, pt: (b, 0, 0))
    grid_spec = pltpu.PrefetchScalarGridSpec(
        num_scalar_prefetch=1,
        grid=(bsz,),
        in_specs=[per_seq(rows, KV_LORA), per_seq(rows, QK_ROPE), per_seq(dec_t, KV_LORA),
                  per_seq(dec_t, QK_ROPE),
                  pl.BlockSpec(memory_space=pl.ANY), pl.BlockSpec(memory_space=pl.ANY)],
        out_specs=per_seq(rows, KV_LORA),
        scratch_shapes=[pltpu.VMEM((2, DEC_REGION_PAGES, page, KV_LORA), F32),
                        pltpu.VMEM((2, DEC_REGION_PAGES, QK_ROPE, page), F32),
                        pltpu.SemaphoreType.DMA((2, 2))],
    )
    return pl.pallas_call(
        functools.partial(_decode_attn_kernel, n_regions, dec_t),
        grid_spec=grid_spec,
        out_shape=jax.ShapeDtypeStruct((bsz, rows, KV_LORA), F32),
        compiler_params=_params(1),
        name="decode_attn",
    )(page_table, qlat, qpe, cnew, penew, cache_ckv, cache_kpe_t)


def _uv_kernel(o_ref, w_uv, att_g, ya_ref):
    y = _dot(o_ref[...].astype(BF16), w_uv[...])
    ya_ref[...] = _rms(y, att_g[...]).astype(BF16)


def _uv_proj(o_cat, w_uv, att_g):
    r = o_cat.shape[0]
    return pl.pallas_call(
        _uv_kernel,
        out_shape=jax.ShapeDtypeStruct((r, D_ATT), BF16),
        compiler_params=pltpu.CompilerParams(vmem_limit_bytes=VMEM_LIMIT),
        name="uv_proj",
    )(o_cat, w_uv, att_g)


def _rope_tables(pos):
    inv = ROPE_THETA ** (-jnp.arange(0, QK_ROPE, 2, dtype=F32) / QK_ROPE)
    ang = pos[:, None] * inv[None, :]
    return jnp.tile(jnp.cos(ang), (1, N_HEADS)), jnp.tile(jnp.sin(ang), (1, N_HEADS))


def kernel(x_prompt, x_sample, cache_ckv, cache_kpe, state_ssm_re, state_ssm_im, page_table, ffn1_pre_g, ffn1_w_gate, ffn1_w_up, ffn1_w_down, ffn1_post_g, mix_pre_g, w_in, ssm_lam_re, ssm_lam_im, ssm_log_step, ssm_b_re, ssm_b_im, ssm_c_re, ssm_c_im, ssm_d, ssm_w_glu, ssm_b_glu, q_norm_g, w_uq, kv_norm_g, w_uk, w_uv, ssm_out_g, att_out_g, w_out, mix_post_g, ffn2_pre_g, ffn2_w_gate, ffn2_w_up, ffn2_w_down, ffn2_post_g):
    assert ffn1_pre_g.shape[0] == 1, "single-layer trunk"
    bsz, seq, _ = x_prompt.shape
    dbs, dseq, _ = x_sample.shape
    past_len = page_table.shape[1] * cache_ckv.shape[2]
    assert bsz == V7X_SUBLANES

    row = lambda v: v[0].reshape(1, -1)
    ffn1_w = (row(ffn1_pre_g), ffn1_w_gate[0].astype(BF16), ffn1_w_up[0].astype(BF16),
              ffn1_w_down[0].astype(BF16), row(ffn1_post_g))
    ffn2_w = (w_out[0, :D_SSM].astype(BF16), w_out[0, D_SSM:].astype(BF16), row(mix_post_g),
              row(ffn2_pre_g), ffn2_w_gate[0].astype(BF16), ffn2_w_up[0].astype(BF16),
              ffn2_w_down[0].astype(BF16), row(ffn2_post_g))
    off_pe = D_SSM + Q_LORA + KV_LORA
    w_in_p = jnp.concatenate(
        [w_in[0, :, :off_pe],
         jnp.tile(w_in[0, :, off_pe:off_pe + HALF_ROPE], (1, N_HEADS)),
         jnp.tile(w_in[0, :, off_pe + HALF_ROPE:], (1, N_HEADS))], axis=1).astype(BF16)
    wq = w_uq[0]
    w_uq_p = jnp.concatenate(
        [wq[:, :, :QK_NOPE].reshape(Q_LORA, N_HEADS * QK_NOPE),
         wq[:, :, QK_NOPE:QK_NOPE + HALF_ROPE].reshape(Q_LORA, N_HEADS * HALF_ROPE),
         wq[:, :, QK_NOPE + HALF_ROPE:].reshape(Q_LORA, N_HEADS * HALF_ROPE)], axis=1).astype(BF16)
    eye_h = jnp.eye(N_HEADS, dtype=F32)
    w_uk_bd = jnp.einsum('hnr,hg->hngr', w_uk[0].transpose(1, 2, 0), eye_h).reshape(
        N_HEADS * QK_NOPE, N_HEADS * KV_LORA).astype(BF16)
    w_uv_bd = jnp.einsum('hrv,hg->hrgv', w_uv[0].transpose(1, 0, 2), eye_h).reshape(
        N_HEADS * KV_LORA, D_ATT).astype(BF16)
    a_re, a_im, bb, cc = _s5_prep(ssm_lam_re[0], ssm_lam_im[0], ssm_log_step[0],
                                  ssm_b_re[0], ssm_b_im[0], ssm_c_re[0], ssm_c_im[0])
    mix_w = (row(mix_pre_g), w_in_p, bb, a_re, a_im, cc, row(ssm_d), ssm_w_glu[0].astype(BF16),
             row(ssm_b_glu), row(ssm_out_g), row(q_norm_g), w_uq_p, w_uk_bd, row(kv_norm_g))
    att_g = row(att_out_g)

    cos_p, sin_p = _rope_tables(jnp.arange(seq, dtype=F32))
    x1_p = _ffn1(x_prompt.reshape(bsz * seq, D_MODEL), ffn1_w)
    ys_p, qlat_p, qpe_p, kcat_p, ckv_p, kpe_p, st_p = _mixer(
        x1_p.reshape(bsz, seq, D_MODEL), cos_p, sin_p, jnp.zeros((bsz, 2 * GP), F32), mix_w,
        bsz, min(MIX_TT, seq))
    ya_p = _prompt_attn(qlat_p, qpe_p, kcat_p, w_uv_bd, att_g)
    y_p = _ffn2(x1_p, ys_p.reshape(bsz * seq, D_SSM), ya_p.reshape(bsz * seq, D_ATT), ffn2_w)

    n_s = dbs * dseq
    cos_s, sin_s = _rope_tables(past_len + jnp.arange(dseq, dtype=F32))
    x1_s = _ffn1(x_sample.reshape(n_s, D_MODEL), ffn1_w)
    h0_s = jnp.concatenate([state_ssm_re[0].reshape(dbs, GP), state_ssm_im[0].reshape(dbs, GP)], axis=1)
    ys_s, qlat_s, qpe_s, _, ckv_s, kpe_s, st_s = _mixer(
        x1_s.reshape(1, n_s, D_MODEL), jnp.tile(cos_s, (dbs, 1)), jnp.tile(sin_s, (dbs, 1)), h0_s, mix_w,
        dbs, dseq)
    qlat_d = qlat_s.reshape(dbs, dseq, N_HEADS, KV_LORA).transpose(0, 2, 1, 3).reshape(
        dbs, N_HEADS * dseq, KV_LORA)
    qpe_d = qpe_s.reshape(dbs, dseq, 2, N_HEADS, HALF_ROPE).transpose(0, 3, 1, 2, 4).reshape(
        dbs, N_HEADS * dseq, QK_ROPE)
    cnew = ckv_s.reshape(dbs, dseq, KV_LORA)
    penew = kpe_s.reshape(dbs, dseq, QK_ROPE)
    o_d = _decode_attn(page_table, qlat_d, qpe_d, cnew, penew, cache_ckv, cache_kpe.transpose(0, 1, 3, 2))
    o_cat = o_d.reshape(dbs, N_HEADS, dseq, KV_LORA).transpose(0, 2, 1, 3).reshape(n_s, N_HEADS * KV_LORA)
    ya_s = _uv_proj(o_cat, w_uv_bd, att_g)
    y_s = _ffn2(x1_s, ys_s.reshape(n_s, D_SSM), ya_s, ffn2_w)

    state = lambda st, n: (st[:, :GP].reshape(1, n, N_GROUPS, SSM_STATE),
                           st[:, GP:].reshape(1, n, N_GROUPS, SSM_STATE))
    re_p, im_p = state(st_p, bsz)
    re_s, im_s = state(st_s, dbs)
    return (y_p.reshape(bsz, seq, D_MODEL), y_s.reshape(dbs, dseq, D_MODEL),
            ckv_p[None], kpe_p[None], re_p, im_p,
            cnew[None], penew[None], re_s, im_s)
```

```python
import functools
import math

import jax
import jax.numpy as jnp
from jax import lax
from jax.experimental import pallas as pl
from jax.experimental.pallas import tpu as pltpu

F32 = jnp.float32
BF16 = jnp.bfloat16

D_MODEL = 1024
D_FF = 2816
EPS = 1e-6
D_SSM = 512
SSM_GROUP = 16
N_GROUPS = D_SSM // SSM_GROUP
SSM_STATE = 64
GP = N_GROUPS * SSM_STATE
N_HEADS = 8
QK_NOPE = 64
QK_ROPE = 32
HALF_ROPE = QK_ROPE // 2
V_DIM = 64
D_ATT = N_HEADS * V_DIM
Q_LORA = 256
KV_LORA = 256
ROPE_THETA = 10000.0
SOFTMAX_SCALE = 1.0 / math.sqrt(QK_NOPE + QK_ROPE)
D_IN_PACKED = D_SSM + Q_LORA + KV_LORA + 2 * N_HEADS * HALF_ROPE
D_Q_PACKED = N_HEADS * QK_NOPE + 2 * N_HEADS * HALF_ROPE
PE_W = 2 * N_HEADS * HALF_ROPE

V7X_SUBLANES = 8
V7X_LANES = 128
MXU_N = 256
VMEM_LIMIT = 56 * 1024 * 1024

FFN_TM = 512
MIX_TT = 64
ATT_TQ = 256
ATT_ROW_CHUNKS = 2
DEC_REGION_PAGES = 64
DEC_SUB_PAGES = 16
SCAN_SLABS = 4
N_SLABS = GP // V7X_LANES


def _const_spec(shape):
    nd = len(shape)
    return pl.BlockSpec(shape, lambda *_: (0,) * nd, pipeline_mode=pl.Buffered(1))


def _rms(x, g):
    return x * lax.rsqrt(jnp.mean(x * x, axis=-1, keepdims=True) + EPS) * g


def _dot(a, b):
    return jnp.dot(a, b, preferred_element_type=F32)


def _dot_nt(a, b):
    return lax.dot_general(a, b, (((1,), (1,)), ((), ())), preferred_element_type=F32)


def _ffn(x, pre_g, wg_ref, wu_ref, wd_ref, post_g):
    h = _rms(x, pre_g).astype(BF16)
    gate = _dot(h, wg_ref[...])
    up = _dot(h, wu_ref[...])
    act = (gate * jax.nn.sigmoid(gate) * up).astype(BF16)
    f = _dot(act, wd_ref[...])
    return x + 0.5 * _rms(f, post_g)


def _params(n_grid_axes):
    return pltpu.CompilerParams(dimension_semantics=("arbitrary",) * n_grid_axes,
                                vmem_limit_bytes=VMEM_LIMIT)


def _ffn1_kernel(x_ref, pre_g, wg, wu, wd, post_g, o_ref):
    o_ref[...] = _ffn(x_ref[...], pre_g[...], wg, wu, wd, post_g[...])


def _ffn_weight_specs():
    return [_const_spec((1, D_MODEL)), _const_spec((D_MODEL, D_FF)), _const_spec((D_MODEL, D_FF)),
            _const_spec((D_FF, D_MODEL)), _const_spec((1, D_MODEL))]


def _ffn1(x, w):
    r = x.shape[0]
    tm = min(FFN_TM, r)
    rows = pl.BlockSpec((tm, D_MODEL), lambda i: (i, 0))
    return pl.pallas_call(
        _ffn1_kernel,
        grid=(r // tm,),
        in_specs=[rows] + _ffn_weight_specs(),
        out_specs=rows,
        out_shape=jax.ShapeDtypeStruct((r, D_MODEL), F32),
        compiler_params=_params(1),
        name="ffn1",
    )(x, *w)


def _ffn2_kernel(x_ref, ys_ref, ya_ref, wo_s, wo_a, mix_g, pre_g, wg, wu, wd, post_g, o_ref):
    mixed = _dot(ys_ref[...], wo_s[...]) + _dot(ya_ref[...], wo_a[...])
    x = x_ref[...] + _rms(mixed, mix_g[...])
    o_ref[...] = _ffn(x, pre_g[...], wg, wu, wd, post_g[...])


def _ffn2(x1, ys, ya, w):
    r = x1.shape[0]
    tm = min(FFN_TM, r)
    rows = lambda width: pl.BlockSpec((tm, width), lambda i: (i, 0))
    weight_specs = [_const_spec((D_SSM, D_MODEL)), _const_spec((D_ATT, D_MODEL)),
                    _const_spec((1, D_MODEL))] + _ffn_weight_specs()
    return pl.pallas_call(
        _ffn2_kernel,
        grid=(r // tm,),
        in_specs=[rows(D_MODEL), rows(D_SSM), rows(D_ATT)] + weight_specs,
        out_specs=rows(D_MODEL),
        out_shape=jax.ShapeDtypeStruct((r, D_MODEL), F32),
        compiler_params=_params(1),
        name="ffn2",
    )(x1, ys, ya, *w)


def _s5_prep_kernel(lr_ref, li_ref, ls_ref, br_ref, bi_ref, cr_ref, ci_ref,
                    are_ref, aim_ref, bb_ref, cc_ref):
    lr = lr_ref[...]
    li = li_ref[...]
    dt = jnp.exp(ls_ref[...])
    mag = jnp.exp(lr * dt)
    a_re = mag * jnp.cos(li * dt)
    a_im = mag * jnp.sin(li * dt)
    den = lr * lr + li * li
    n_re = a_re - 1.0
    z_re = (n_re * lr + a_im * li) / den
    z_im = (a_im * lr - n_re * li) / den
    are_ref[...] = a_re
    aim_ref[...] = a_im

    row_g = lax.shift_right_logical(lax.broadcasted_iota(jnp.int32, (D_SSM, GP), 0), 4)
    col_g = lax.shift_right_logical(lax.broadcasted_iota(jnp.int32, (D_SSM, GP), 1), 6)
    same = row_g == col_g
    br = br_ref[...]
    bi = bi_ref[...]
    bb_ref[:, :GP] = jnp.where(same, z_re * br - z_im * bi, 0.0).astype(BF16)
    bb_ref[:, GP:] = jnp.where(same, z_re * bi + z_im * br, 0.0).astype(BF16)

    row_g = lax.shift_right_logical(lax.broadcasted_iota(jnp.int32, (GP, D_SSM), 0), 6)
    col_g = lax.shift_right_logical(lax.broadcasted_iota(jnp.int32, (GP, D_SSM), 1), 4)
    same = row_g == col_g
    cc_ref[:GP, :] = jnp.where(same, cr_ref[...], 0.0).astype(BF16)
    cc_ref[GP:, :] = jnp.where(same, -ci_ref[...], 0.0).astype(BF16)


def _s5_prep(lam_re, lam_im, log_step, b_re, b_im, c_re, c_im):
    lr = lam_re.reshape(1, GP)
    li = lam_im.reshape(1, GP)
    ls = jnp.repeat(log_step, SSM_STATE).reshape(1, GP)
    br = jnp.tile(b_re.transpose(2, 0, 1).reshape(SSM_GROUP, GP), (N_GROUPS, 1))
    bi = jnp.tile(b_im.transpose(2, 0, 1).reshape(SSM_GROUP, GP), (N_GROUPS, 1))
    cr = jnp.tile(c_re.transpose(0, 2, 1).reshape(GP, SSM_GROUP), (1, N_GROUPS))
    ci = jnp.tile(c_im.transpose(0, 2, 1).reshape(GP, SSM_GROUP), (1, N_GROUPS))
    return pl.pallas_call(
        _s5_prep_kernel,
        out_shape=(jax.ShapeDtypeStruct((1, GP), F32), jax.ShapeDtypeStruct((1, GP), F32),
                   jax.ShapeDtypeStruct((D_SSM, 2 * GP), BF16), jax.ShapeDtypeStruct((2 * GP, D_SSM), BF16)),
        compiler_params=pltpu.CompilerParams(vmem_limit_bytes=VMEM_LIMIT),
        name="s5_prep",
    )(lr, li, ls, br, bi, cr, ci)


def _scan_pitch(tt):
    return tt if tt % 16 else tt + V7X_SUBLANES


def _mixer_kernel(nb, tt,
                  x_ref, cos_ref, sin_ref, h0_ref, pre_g, w_in, bb, are_ref, aim_ref, cc, d_skip,
                  w_glu, b_glu, ssm_g, q_g, w_uq, w_uk, kv_g,
                  ys_ref, qlat_ref, qpe_ref, kcat_ref, ckv_ref, kpe_ref, st_ref,
                  hbuf):
    g, rpg, _ = x_ref.shape
    r = nb * tt
    pitch = _scan_pitch(tt)

    @pl.when(pl.program_id(0) == 0)
    def _():
        st_ref[...] = h0_ref[...]

    x = x_ref[...].reshape(r, D_MODEL)
    h = _rms(x, pre_g[...]).astype(BF16)
    z = _dot(h, w_in[...])
    u = z[:, :D_SSM]
    hq = z[:, D_SSM:D_SSM + Q_LORA]
    hkv = z[:, D_SSM + Q_LORA:D_SSM + Q_LORA + KV_LORA]
    hpe = z[:, D_SSM + Q_LORA + KV_LORA:]

    u_bf = u.astype(BF16)
    tiles_per_part = GP // MXU_N
    for j in range(2 * tiles_per_part):
        k0 = ((j % tiles_per_part) // 2) * V7X_LANES
        bu = _dot(u_bf[:, k0:k0 + V7X_LANES], bb[k0:k0 + V7X_LANES, j * MXU_N:(j + 1) * MXU_N])
        for half in range(MXU_N // V7X_LANES):
            c = j * (MXU_N // V7X_LANES) + half
            lanes = slice(half * V7X_LANES, (half + 1) * V7X_LANES)
            if pitch == tt:
                hbuf[c] = bu[:, lanes]
            else:
                for b in range(nb):
                    hbuf[c, b * pitch:b * pitch + tt, :] = bu[b * tt:(b + 1) * tt, lanes]

    n_sub = nb // V7X_SUBLANES
    for lc in range(N_SLABS // SCAN_SLABS):
        slabs = [lc * SCAN_SLABS + k for k in range(SCAN_SLABS)]
        a_re = [are_ref[:, s * V7X_LANES:(s + 1) * V7X_LANES] for s in slabs]
        a_im = [aim_ref[:, s * V7X_LANES:(s + 1) * V7X_LANES] for s in slabs]

        def sub_body(sg, carry, slabs=slabs, a_re=a_re, a_im=a_im):
            seqs = pl.ds(pl.multiple_of(sg * V7X_SUBLANES, V7X_SUBLANES), V7X_SUBLANES)
            row0 = sg * (V7X_SUBLANES * pitch)

            def t_body(t, state):
                rows = pl.ds(row0 + t, V7X_SUBLANES, stride=pitch)
                new = []
                for k, s in enumerate(slabs):
                    s_re, s_im = state[2 * k], state[2 * k + 1]
                    n_re = a_re[k] * s_re - a_im[k] * s_im + hbuf[s, rows, :]
                    n_im = a_re[k] * s_im + a_im[k] * s_re + hbuf[N_SLABS + s, rows, :]
                    hbuf[s, rows, :] = n_re
                    hbuf[N_SLABS + s, rows, :] = n_im
                    new += [n_re, n_im]
                return tuple(new)

            init = []
            for s in slabs:
                init += [st_ref[seqs, s * V7X_LANES:(s + 1) * V7X_LANES],
                         st_ref[seqs, GP + s * V7X_LANES:GP + (s + 1) * V7X_LANES]]
            fin = lax.fori_loop(0, tt, t_body, tuple(init), unroll=min(tt, 4))
            for k, s in enumerate(slabs):
                st_ref[seqs, s * V7X_LANES:(s + 1) * V7X_LANES] = fin[2 * k]
                st_ref[seqs, GP + s * V7X_LANES:GP + (s + 1) * V7X_LANES] = fin[2 * k + 1]
            return carry

        lax.fori_loop(0, n_sub, sub_body, 0)

    def slab_rows(c):
        if pitch == tt:
            return hbuf[c]
        return jnp.concatenate([hbuf[c, b * pitch:b * pitch + tt, :] for b in range(nb)], axis=0)

    y_tiles = []
    k_per_tile = GP * MXU_N // D_SSM
    slabs_per_tile = k_per_tile // V7X_LANES
    for n in range(D_SSM // MXU_N):
        cols = slice(n * MXU_N, (n + 1) * MXU_N)
        acc = None
        for part in range(2):
            s0 = part * N_SLABS + n * slabs_per_tile
            hs = jnp.concatenate([slab_rows(s0 + c) for c in range(slabs_per_tile)], axis=1).astype(BF16)
            k0 = part * GP + n * k_per_tile
            term = _dot(hs, cc[k0:k0 + k_per_tile, cols])
            acc = term if acc is None else acc + term
        y_tiles.append(acc)
    y = jnp.concatenate(y_tiles, axis=1) + d_skip[...] * u
    y = jax.nn.gelu(y, approximate=True)
    y = y * jax.nn.sigmoid(_dot(y.astype(BF16), w_glu[...]) + b_glu[...])
    ys_ref[...] = _rms(y, ssm_g[...]).astype(BF16).reshape(g, rpg, D_SSM)

    cos = cos_ref[...]
    sin = sin_ref[...]

    def rope(x1, x2):
        x1 = x1.reshape(g, rpg, V7X_LANES)
        x2 = x2.reshape(g, rpg, V7X_LANES)
        return x1 * cos - x2 * sin, x2 * cos + x1 * sin

    q = _dot(_rms(hq, q_g[...]).astype(BF16), w_uq[...]) * SOFTMAX_SCALE
    q_nope = q[:, :N_HEADS * QK_NOPE]
    q1, q2 = rope(q[:, N_HEADS * QK_NOPE:N_HEADS * QK_NOPE + V7X_LANES],
                  q[:, N_HEADS * QK_NOPE + V7X_LANES:])
    qpe_ref[:, :, :V7X_LANES] = q1.astype(BF16)
    qpe_ref[:, :, V7X_LANES:] = q2.astype(BF16)
    q_nope = q_nope.astype(BF16)
    for hd in range(N_HEADS):
        k0 = (hd * QK_NOPE // V7X_LANES) * V7X_LANES
        cols = slice(hd * KV_LORA, (hd + 1) * KV_LORA)
        qlat_ref[:, :, cols] = _dot(q_nope[:, k0:k0 + V7X_LANES], w_uk[k0:k0 + V7X_LANES, cols]).astype(
            BF16).reshape(g, rpg, KV_LORA)

    ckv = _rms(hkv, kv_g[...]).reshape(g, rpg, KV_LORA)
    ckv_ref[...] = ckv
    k1, k2 = rope(hpe[:, :V7X_LANES], hpe[:, V7X_LANES:])
    kcat_ref[:, :, :KV_LORA] = ckv.astype(BF16)
    kcat_ref[:, :, KV_LORA:KV_LORA + V7X_LANES] = k1.astype(BF16)
    kcat_ref[:, :, KV_LORA + V7X_LANES:] = k2.astype(BF16)
    kpe_ref[:, :, :HALF_ROPE] = k1[:, :, :HALF_ROPE]
    kpe_ref[:, :, HALF_ROPE:] = k2[:, :, :HALF_ROPE]


def _mixer(x1, cos, sin, h0, w, nb, tt):
    n_g, length, _ = x1.shape
    rpg = nb * tt // n_g
    assert length % rpg == 0 and rpg % 16 == 0 and nb % V7X_SUBLANES == 0
    pitch = _scan_pitch(tt)
    row = lambda width: pl.BlockSpec((n_g, rpg, width), lambda i: (0, i, 0))
    out = lambda width, dtype: jax.ShapeDtypeStruct((n_g, length, width), dtype)
    weight_specs = [
        _const_spec((1, D_MODEL)), _const_spec((D_MODEL, D_IN_PACKED)), _const_spec((D_SSM, 2 * GP)),
        _const_spec((1, GP)), _const_spec((1, GP)), _const_spec((2 * GP, D_SSM)), _const_spec((1, D_SSM)),
        _const_spec((D_SSM, D_SSM)), _const_spec((1, D_SSM)), _const_spec((1, D_SSM)),
        _const_spec((1, Q_LORA)), _const_spec((Q_LORA, D_Q_PACKED)),
        _const_spec((N_HEADS * QK_NOPE, N_HEADS * KV_LORA)), _const_spec((1, KV_LORA)),
    ]
    rope_spec = pl.BlockSpec((rpg, V7X_LANES), lambda i: (i, 0))
    return pl.pallas_call(
        functools.partial(_mixer_kernel, nb, tt),
        grid=(length // rpg,),
        in_specs=[row(D_MODEL), rope_spec, rope_spec, _const_spec((nb, 2 * GP))] + weight_specs,
        out_specs=[row(D_SSM), row(N_HEADS * KV_LORA), row(PE_W), row(KV_LORA + PE_W), row(KV_LORA),
                   row(QK_ROPE), pl.BlockSpec((nb, 2 * GP), lambda i: (0, 0))],
        out_shape=[out(D_SSM, BF16), out(N_HEADS * KV_LORA, BF16), out(PE_W, BF16),
                   out(KV_LORA + PE_W, BF16), out(KV_LORA, F32), out(QK_ROPE, F32),
                   jax.ShapeDtypeStruct((nb, 2 * GP), F32)],
        scratch_shapes=[pltpu.VMEM((2 * N_SLABS, nb * pitch, V7X_LANES), F32)],
        compiler_params=_params(1),
        name="mixer",
    )(x1, cos, sin, h0, *w)


def _prompt_attn_kernel(tq, qlat_ref, qpe_ref, kcat_ref, w_uv, att_g, ya_ref, acc_ref):
    i = pl.program_id(1)
    qpe = qpe_ref[...]
    head_of_lane = lax.shift_right_logical(
        lax.broadcasted_iota(jnp.int32, (tq, PE_W), 1) & (V7X_LANES - 1), 4)
    q_heads = []
    for hd in range(N_HEADS):
        q_heads.append(jnp.concatenate(
            [qlat_ref[:, hd * KV_LORA:(hd + 1) * KV_LORA],
             jnp.where(head_of_lane == hd, qpe, jnp.zeros_like(qpe))], axis=-1))
    hpc = N_HEADS // ATT_ROW_CHUNKS
    rows = hpc * tq
    qs = [jnp.concatenate(q_heads[c * hpc:(c + 1) * hpc], axis=0) for c in range(ATT_ROW_CHUNKS)]

    def block(j, state, masked):
        k = kcat_ref[pl.ds(pl.multiple_of(j * tq, tq), tq), :]
        v = k[:, :KV_LORA]
        scores = [_dot_nt(q, k) for q in qs]
        new_state = []
        for c, s in enumerate(scores):
            m, l = state[2 * c], state[2 * c + 1]
            if masked:
                q_pos = lax.broadcasted_iota(jnp.int32, (rows, tq), 0) & (tq - 1)
                k_pos = lax.broadcasted_iota(jnp.int32, (rows, tq), 1)
                s = jnp.where(k_pos <= q_pos, s, -jnp.inf)
            m_new = jnp.maximum(m, jnp.max(s, axis=-1, keepdims=True))
            alpha = jnp.exp(m - m_new)
            p = jnp.exp(s - m_new)
            l_new = alpha * l + jnp.sum(p, axis=-1, keepdims=True)
            acc_rows = pl.ds(c * rows, rows)
            acc_ref[acc_rows, :] = alpha * acc_ref[acc_rows, :] + _dot(p.astype(BF16), v)
            new_state += [m_new, l_new]
        return tuple(new_state)

    acc_ref[...] = jnp.zeros_like(acc_ref)
    init = (jnp.full((rows, 1), -jnp.inf, F32), jnp.zeros((rows, 1), F32)) * ATT_ROW_CHUNKS
    state = block(i, init, True)
    state = lax.fori_loop(0, i, lambda j, st: block(j, st, False), state)

    l_all = jnp.concatenate([state[2 * c + 1] for c in range(ATT_ROW_CHUNKS)], axis=0)
    o = acc_ref[...] / l_all
    o_cat = jnp.concatenate([o[hd * tq:(hd + 1) * tq, :] for hd in range(N_HEADS)], axis=-1)
    y = _dot(o_cat.astype(BF16), w_uv[...])
    ya_ref[...] = _rms(y, att_g[...]).astype(BF16)


def _prompt_attn(qlat, qpe, kcat, w_uv, att_g):
    bsz, t, _ = qlat.shape
    tq = min(ATT_TQ, t)
    assert t % tq == 0 and tq & (tq - 1) == 0
    return pl.pallas_call(
        functools.partial(_prompt_attn_kernel, tq),
        grid=(bsz, t // tq),
        in_specs=[pl.BlockSpec((None, tq, N_HEADS * KV_LORA), lambda b, i: (b, i, 0)),
                  pl.BlockSpec((None, tq, PE_W), lambda b, i: (b, i, 0)),
                  pl.BlockSpec((None, t, KV_LORA + PE_W), lambda b, i: (b, 0, 0)),
                  _const_spec((N_HEADS * KV_LORA, D_ATT)), _const_spec((1, D_ATT))],
        out_specs=pl.BlockSpec((None, tq, D_ATT), lambda b, i: (b, i, 0)),
        out_shape=jax.ShapeDtypeStruct((bsz, t, D_ATT), BF16),
        scratch_shapes=[pltpu.VMEM((N_HEADS * tq, KV_LORA), F32)],
        compiler_params=_params(2),
        name="prompt_attn",
    )(qlat, qpe, kcat, w_uv, att_g)


def _merge(m, l, acc, m_c, l_c, acc_c):
    m_new = jnp.maximum(m, m_c)
    w = jnp.exp(m - m_new)
    w_c = jnp.exp(m_c - m_new)
    return m_new, w * l + w_c * l_c, w * acc + w_c * acc_c


def _decode_attn_kernel(n_regions, dec_t,
                        pt_ref, qlat_ref, qpe_ref, cnew_ref, penew_ref, ckv_hbm, kpe_hbm,
                        o_ref, kvbuf, pebuf, sem):
    b = pl.program_id(0)
    nb = pl.num_programs(0)
    rows = N_HEADS * dec_t

    def page_copies(bb, region, slot, p):
        page = pt_ref[bb, region * DEC_REGION_PAGES + p]
        return (pltpu.make_async_copy(ckv_hbm.at[0, page], kvbuf.at[slot, p], sem.at[0, slot]),
                pltpu.make_async_copy(kpe_hbm.at[0, page], pebuf.at[slot, p], sem.at[1, slot]))

    def start_region(bb, region, slot):
        for p in range(DEC_REGION_PAGES):
            for cp in page_copies(bb, region, slot, p):
                cp.start()

    def wait_region(bb, region, slot):
        for p in range(DEC_REGION_PAGES):
            for cp in page_copies(bb, region, slot, p):
                cp.wait()

    @pl.when(b == 0)
    def _():
        start_region(0, 0, 0)

    qlat = qlat_ref[...]
    qpe = qpe_ref[...]
    qlat_f = qlat.astype(F32)
    qpe_f = qpe.astype(F32)

    t_of_row = lax.broadcasted_iota(jnp.int32, (rows, 1), 0) & (dec_t - 1)
    s_new = []
    for j in range(dec_t):
        sj = (jnp.sum(qlat_f * cnew_ref[j:j + 1, :], axis=-1, keepdims=True)
              + jnp.sum(qpe_f * penew_ref[j:j + 1, :], axis=-1, keepdims=True))
        s_new.append(jnp.where(t_of_row >= j, sj, -jnp.inf))
    m = s_new[0]
    for j in range(1, dec_t):
        m = jnp.maximum(m, s_new[j])
    l = jnp.zeros((rows, 1), F32)
    acc = jnp.zeros((rows, KV_LORA), F32)
    for j in range(dec_t):
        pj = jnp.exp(s_new[j] - m)
        l = l + pj
        acc = acc + pj * cnew_ref[j:j + 1, :]

    sub_keys = DEC_SUB_PAGES * V7X_LANES
    for region in range(n_regions):
        slot = region % 2
        if region + 1 < n_regions:
            start_region(b, region + 1, 1 - slot)
        else:
            @pl.when(b + 1 < nb)
            def _():
                start_region(b + 1, 0, 1 - slot)
        wait_region(b, region, slot)
        kvs, scores = [], []
        for sub in range(DEC_REGION_PAGES // DEC_SUB_PAGES):
            p0 = sub * DEC_SUB_PAGES
            kv = kvbuf[slot, p0:p0 + DEC_SUB_PAGES].reshape(sub_keys, KV_LORA).astype(BF16)
            pe_t = jnp.concatenate([pebuf[slot, p0 + p] for p in range(DEC_SUB_PAGES)],
                                   axis=1).astype(BF16)
            kvs.append(kv)
            scores.append(_dot_nt(qlat, kv) + _dot(qpe, pe_t))
        for kv, s in zip(kvs, scores):
            m_c = jnp.max(s, axis=-1, keepdims=True)
            p = jnp.exp(s - m_c)
            l_c = jnp.sum(p, axis=-1, keepdims=True)
            acc_c = _dot(p.astype(BF16), kv)
            m, l, acc = _merge(m, l, acc, m_c, l_c, acc_c)

    o_ref[...] = acc / l


def _decode_attn(page_table, qlat, qpe, cnew, penew, cache_ckv, cache_kpe_t):
    bsz, n_pages = page_table.shape
    dec_t = cnew.shape[1]
    page = cache_ckv.shape[2]
    assert page == V7X_LANES and n_pages % (2 * DEC_REGION_PAGES) == 0 and dec_t & (dec_t - 1) == 0
    rows = N_HEADS * dec_t
    n_regions = n_pages // DEC_REGION_PAGES
    per_seq = lambda r, width: pl.BlockSpec((None, r, width), lambda b, pt: (b, 0, 0))
    grid_spec = pltpu.PrefetchScalarGridSpec(
        num_scalar_prefetch=1,
        grid=(bsz,),
        in_specs=[per_seq(rows, KV_LORA), per_seq(rows, QK_ROPE), per_seq(dec_t, KV_LORA),
                  per_seq(dec_t, QK_ROPE),
                  pl.BlockSpec(memory_space=pl.ANY), pl.BlockSpec(memory_space=pl.ANY)],
        out_specs=per_seq(rows, KV_LORA),
        scratch_shapes=[pltpu.VMEM((2, DEC_REGION_PAGES, page, KV_LORA), F32),
                        pltpu.VMEM((2, DEC_REGION_PAGES, QK_ROPE, page), F32),
                        pltpu.SemaphoreType.DMA((2, 2))],
    )
    return pl.pallas_call(
        functools.partial(_decode_attn_kernel, n_regions, dec_t),
        grid_spec=grid_spec,
        out_shape=jax.ShapeDtypeStruct((bsz, rows, KV_LORA), F32),
        compiler_params=_params(1),
        name="decode_attn",
    )(page_table, qlat, qpe, cnew, penew, cache_ckv, cache_kpe_t)


def _uv_kernel(o_ref, w_uv, att_g, ya_ref):
    y = _dot(o_ref[...].astype(BF16), w_uv[...])
    ya_ref[...] = _rms(y, att_g[...]).astype(BF16)


def _uv_proj(o_cat, w_uv, att_g):
    r = o_cat.shape[0]
    return pl.pallas_call(
        _uv_kernel,
        out_shape=jax.ShapeDtypeStruct((r, D_ATT), BF16),
        compiler_params=pltpu.CompilerParams(vmem_limit_bytes=VMEM_LIMIT),
        name="uv_proj",
    )(o_cat, w_uv, att_g)


def _rope_tables(pos):
    inv = ROPE_THETA ** (-jnp.arange(0, QK_ROPE, 2, dtype=F32) / QK_ROPE)
    ang = pos[:, None] * inv[None, :]
    return jnp.tile(jnp.cos(ang), (1, N_HEADS)), jnp.tile(jnp.sin(ang), (1, N_HEADS))


def kernel(x_prompt, x_sample, cache_ckv, cache_kpe, state_ssm_re, state_ssm_im, page_table, ffn1_pre_g, ffn1_w_gate, ffn1_w_up, ffn1_w_down, ffn1_post_g, mix_pre_g, w_in, ssm_lam_re, ssm_lam_im, ssm_log_step, ssm_b_re, ssm_b_im, ssm_c_re, ssm_c_im, ssm_d, ssm_w_glu, ssm_b_glu, q_norm_g, w_uq, kv_norm_g, w_uk, w_uv, ssm_out_g, att_out_g, w_out, mix_post_g, ffn2_pre_g, ffn2_w_gate, ffn2_w_up, ffn2_w_down, ffn2_post_g):
    assert ffn1_pre_g.shape[0] == 1, "single-layer trunk"
    bsz, seq, _ = x_prompt.shape
    dbs, dseq, _ = x_sample.shape
    past_len = page_table.shape[1] * cache_ckv.shape[2]
    assert bsz == V7X_SUBLANES

    row = lambda v: v[0].reshape(1, -1)
    ffn1_w = (row(ffn1_pre_g), ffn1_w_gate[0].astype(BF16), ffn1_w_up[0].astype(BF16),
              ffn1_w_down[0].astype(BF16), row(ffn1_post_g))
    ffn2_w = (w_out[0, :D_SSM].astype(BF16), w_out[0, D_SSM:].astype(BF16), row(mix_post_g),
              row(ffn2_pre_g), ffn2_w_gate[0].astype(BF16), ffn2_w_up[0].astype(BF16),
              ffn2_w_down[0].astype(BF16), row(ffn2_post_g))
    off_pe = D_SSM + Q_LORA + KV_LORA
    w_in_p = jnp.concatenate(
        [w_in[0, :, :off_pe],
         jnp.tile(w_in[0, :, off_pe:off_pe + HALF_ROPE], (1, N_HEADS)),
         jnp.tile(w_in[0, :, off_pe + HALF_ROPE:], (1, N_HEADS))], axis=1).astype(BF16)
    wq = w_uq[0]
    w_uq_p = jnp.concatenate(
        [wq[:, :, :QK_NOPE].reshape(Q_LORA, N_HEADS * QK_NOPE),
         wq[:, :, QK_NOPE:QK_NOPE + HALF_ROPE].reshape(Q_LORA, N_HEADS * HALF_ROPE),
         wq[:, :, QK_NOPE + HALF_ROPE:].reshape(Q_LORA, N_HEADS * HALF_ROPE)], axis=1).astype(BF16)
    eye_h = jnp.eye(N_HEADS, dtype=F32)
    w_uk_bd = jnp.einsum('hnr,hg->hngr', w_uk[0].transpose(1, 2, 0), eye_h).reshape(
        N_HEADS * QK_NOPE, N_HEADS * KV_LORA).astype(BF16)
    w_uv_bd = jnp.einsum('hrv,hg->hrgv', w_uv[0].transpose(1, 0, 2), eye_h).reshape(
        N_HEADS * KV_LORA, D_ATT).astype(BF16)
    a_re, a_im, bb, cc = _s5_prep(ssm_lam_re[0], ssm_lam_im[0], ssm_log_step[0],
                                  ssm_b_re[0], ssm_b_im[0], ssm_c_re[0], ssm_c_im[0])
    mix_w = (row(mix_pre_g), w_in_p, bb, a_re, a_im, cc, row(ssm_d), ssm_w_glu[0].astype(BF16),
             row(ssm_b_glu), row(ssm_out_g), row(q_norm_g), w_uq_p, w_uk_bd, row(kv_norm_g))
    att_g = row(att_out_g)

    cos_p, sin_p = _rope_tables(jnp.arange(seq, dtype=F32))
    x1_p = _ffn1(x_prompt.reshape(bsz * seq, D_MODEL), ffn1_w)
    ys_p, qlat_p, qpe_p, kcat_p, ckv_p, kpe_p, st_p = _mixer(
        x1_p.reshape(bsz, seq, D_MODEL), cos_p, sin_p, jnp.zeros((bsz, 2 * GP), F32), mix_w,
        bsz, min(MIX_TT, seq))
    ya_p = _prompt_attn(qlat_p, qpe_p, kcat_p, w_uv_bd, att_g)
    y_p = _ffn2(x1_p, ys_p.reshape(bsz * seq, D_SSM), ya_p.reshape(bsz * seq, D_ATT), ffn2_w)

    n_s = dbs * dseq
    cos_s, sin_s = _rope_tables(past_len + jnp.arange(dseq, dtype=F32))
    x1_s = _ffn1(x_sample.reshape(n_s, D_MODEL), ffn1_w)
    h0_s = jnp.concatenate([state_ssm_re[0].reshape(dbs, GP), state_ssm_im[0].reshape(dbs, GP)], axis=1)
    ys_s, qlat_s, qpe_s, _, ckv_s, kpe_s, st_s = _mixer(
        x1_s.reshape(1, n_s, D_MODEL), jnp.tile(cos_s, (dbs, 1)), jnp.tile(sin_s, (dbs, 1)), h0_s, mix_w,
        dbs, dseq)
    qlat_d = qlat_s.reshape(dbs, dseq, N_HEADS, KV_LORA).transpose(0, 2, 1, 3).reshape(
        dbs, N_HEADS * dseq, KV_LORA)
    qpe_d = qpe_s.reshape(dbs, dseq, 2, N_HEADS, HALF_ROPE).transpose(0, 3, 1, 2, 4).reshape(
        dbs, N_HEADS * dseq, QK_ROPE)
    cnew = ckv_s.reshape(dbs, dseq, KV_LORA)
    penew = kpe_s.reshape(dbs, dseq, QK_ROPE)
    o_d = _decode_attn(page_table, qlat_d, qpe_d, cnew, penew, cache_ckv, cache_kpe.transpose(0, 1, 3, 2))
    o_cat = o_d.reshape(dbs, N_HEADS, dseq, KV_LORA).transpose(0, 2, 1, 3).reshape(n_s, N_HEADS * KV_LORA)
    ya_s = _uv_proj(o_cat, w_uv_bd, att_g)
    y_s = _ffn2(x1_s, ys_s.reshape(n_s, D_SSM), ya_s, ffn2_w)

    state = lambda st, n: (st[:, :GP].reshape(1, n, N_GROUPS, SSM_STATE),
                           st[:, GP:].reshape(1, n, N_GROUPS, SSM_STATE))
    re_p, im_p = state(st_p, bsz)
    re_s, im_s = state(st_s, dbs)
    return (y_p.reshape(bsz, seq, D_MODEL), y_s.reshape(dbs, dseq, D_MODEL),
            ckv_p[None], kpe_p[None], re_p, im_p,
            cnew[None], penew[None], re_s, im_s)
```

```python
import functools
import math

import jax
import jax.numpy as jnp
from jax import lax
from jax.experimental import pallas as pl
from jax.experimental.pallas import tpu as pltpu

F32 = jnp.float32
BF16 = jnp.bfloat16

D_MODEL = 1024
D_FF = 2816
EPS = 1e-6
D_SSM = 512
SSM_GROUP = 16
N_GROUPS = D_SSM // SSM_GROUP
SSM_STATE = 64
GP = N_GROUPS * SSM_STATE
N_HEADS = 8
QK_NOPE = 64
QK_ROPE = 32
HALF_ROPE = QK_ROPE // 2
V_DIM = 64
D_ATT = N_HEADS * V_DIM
Q_LORA = 256
KV_LORA = 256
ROPE_THETA = 10000.0
SOFTMAX_SCALE = 1.0 / math.sqrt(QK_NOPE + QK_ROPE)
D_IN_PACKED = D_SSM + Q_LORA + KV_LORA + 2 * N_HEADS * HALF_ROPE
D_Q_PACKED = N_HEADS * QK_NOPE + 2 * N_HEADS * HALF_ROPE
PE_W = 2 * N_HEADS * HALF_ROPE

V7X_SUBLANES = 8
V7X_LANES = 128
MXU_N = 256
VMEM_LIMIT = 56 * 1024 * 1024

FFN_TM = 512
FFN_SUB_TILES = 2
MIX_TT = 64
ATT_TQ = 256
ATT_ROW_CHUNKS = 4
DEC_REGION_PAGES = 64
DEC_SUB_PAGES = 16
SCAN_SLABS = 4
N_SLABS = GP // V7X_LANES


def _const_spec(shape):
    nd = len(shape)
    return pl.BlockSpec(shape, lambda *_: (0,) * nd, pipeline_mode=pl.Buffered(1))


def _rms(x, g):
    return x * lax.rsqrt(jnp.mean(x * x, axis=-1, keepdims=True) + EPS) * g


def _dot(a, b):
    return jnp.dot(a, b, preferred_element_type=F32)


def _dot_nt(a, b):
    return lax.dot_general(a, b, (((1,), (1,)), ((), ())), preferred_element_type=F32)


def _ffn(xs, pre_g, wg_ref, wu_ref, wd_ref, post_g):
    hs = [_rms(x, pre_g).astype(BF16) for x in xs]
    gates = [_dot(h, wg_ref[...]) for h in hs]
    ups = [_dot(h, wu_ref[...]) for h in hs]
    acts = [(gate * jax.nn.sigmoid(gate) * up).astype(BF16) for gate, up in zip(gates, ups)]
    fs = [_dot(act, wd_ref[...]) for act in acts]
    return [x + 0.5 * _rms(f, post_g) for x, f in zip(xs, fs)]


def _params(n_grid_axes):
    return pltpu.CompilerParams(dimension_semantics=("arbitrary",) * n_grid_axes,
                                vmem_limit_bytes=VMEM_LIMIT)


def _sub_tiles(n_rows):
    sub = n_rows // FFN_SUB_TILES if n_rows % (FFN_SUB_TILES * 16) == 0 else n_rows
    return [pl.ds(r0, sub) for r0 in range(0, n_rows, sub)]


def _ffn1_kernel(x_ref, pre_g, wg, wu, wd, post_g, o_ref):
    tiles = _sub_tiles(x_ref.shape[0])
    outs = _ffn([x_ref[rows, :] for rows in tiles], pre_g[...], wg, wu, wd, post_g[...])
    for rows, out in zip(tiles, outs):
        o_ref[rows, :] = out


def _ffn_weight_specs():
    return [_const_spec((1, D_MODEL)), _const_spec((D_MODEL, D_FF)), _const_spec((D_MODEL, D_FF)),
            _const_spec((D_FF, D_MODEL)), _const_spec((1, D_MODEL))]


def _ffn1(x, w):
    r = x.shape[0]
    tm = min(FFN_TM, r)
    rows = pl.BlockSpec((tm, D_MODEL), lambda i: (i, 0))
    return pl.pallas_call(
        _ffn1_kernel,
        grid=(r // tm,),
        in_specs=[rows] + _ffn_weight_specs(),
        out_specs=rows,
        out_shape=jax.ShapeDtypeStruct((r, D_MODEL), F32),
        compiler_params=_params(1),
        name="ffn1",
    )(x, *w)


def _ffn2_kernel(x_ref, ys_ref, ya_ref, wo_s, wo_a, mix_g, pre_g, wg, wu, wd, post_g, o_ref):
    tiles = _sub_tiles(x_ref.shape[0])
    xs = []
    for rows in tiles:
        mixed = _dot(ys_ref[rows, :], wo_s[...]) + _dot(ya_ref[rows, :], wo_a[...])
        xs.append(x_ref[rows, :] + _rms(mixed, mix_g[...]))
    outs = _ffn(xs, pre_g[...], wg, wu, wd, post_g[...])
    for rows, out in zip(tiles, outs):
        o_ref[rows, :] = out


def _ffn2(x1, ys, ya, w):
    r = x1.shape[0]
    tm = min(FFN_TM, r)
    rows = lambda width: pl.BlockSpec((tm, width), lambda i: (i, 0))
    weight_specs = [_const_spec((D_SSM, D_MODEL)), _const_spec((D_ATT, D_MODEL)),
                    _const_spec((1, D_MODEL))] + _ffn_weight_specs()
    return pl.pallas_call(
        _ffn2_kernel,
        grid=(r // tm,),
        in_specs=[rows(D_MODEL), rows(D_SSM), rows(D_ATT)] + weight_specs,
        out_specs=rows(D_MODEL),
        out_shape=jax.ShapeDtypeStruct((r, D_MODEL), F32),
        compiler_params=_params(1),
        name="ffn2",
    )(x1, ys, ya, *w)


def _s5_prep_kernel(lr_ref, li_ref, ls_ref, br_ref, bi_ref, cr_ref, ci_ref,
                    are_ref, aim_ref, bb_ref, cc_ref):
    lr = lr_ref[...]
    li = li_ref[...]
    dt = jnp.exp(ls_ref[...])
    mag = jnp.exp(lr * dt)
    a_re = mag * jnp.cos(li * dt)
    a_im = mag * jnp.sin(li * dt)
    den = lr * lr + li * li
    n_re = a_re - 1.0
    z_re = (n_re * lr + a_im * li) / den
    z_im = (a_im * lr - n_re * li) / den
    are_ref[...] = a_re
    aim_ref[...] = a_im

    row_g = lax.shift_right_logical(lax.broadcasted_iota(jnp.int32, (D_SSM, GP), 0), 4)
    col_g = lax.shift_right_logical(lax.broadcasted_iota(jnp.int32, (D_SSM, GP), 1), 6)
    same = row_g == col_g
    br = br_ref[...]
    bi = bi_ref[...]
    bb_ref[:, :GP] = jnp.where(same, z_re * br - z_im * bi, 0.0).astype(BF16)
    bb_ref[:, GP:] = jnp.where(same, z_re * bi + z_im * br, 0.0).astype(BF16)

    row_g = lax.shift_right_logical(lax.broadcasted_iota(jnp.int32, (GP, D_SSM), 0), 6)
    col_g = lax.shift_right_logical(lax.broadcasted_iota(jnp.int32, (GP, D_SSM), 1), 4)
    same = row_g == col_g
    cc_ref[:GP, :] = jnp.where(same, cr_ref[...], 0.0).astype(BF16)
    cc_ref[GP:, :] = jnp.where(same, -ci_ref[...], 0.0).astype(BF16)


def _s5_prep(lam_re, lam_im, log_step, b_re, b_im, c_re, c_im):
    lr = lam_re.reshape(1, GP)
    li = lam_im.reshape(1, GP)
    ls = jnp.repeat(log_step, SSM_STATE).reshape(1, GP)
    br = jnp.tile(b_re.transpose(2, 0, 1).reshape(SSM_GROUP, GP), (N_GROUPS, 1))
    bi = jnp.tile(b_im.transpose(2, 0, 1).reshape(SSM_GROUP, GP), (N_GROUPS, 1))
    cr = jnp.tile(c_re.transpose(0, 2, 1).reshape(GP, SSM_GROUP), (1, N_GROUPS))
    ci = jnp.tile(c_im.transpose(0, 2, 1).reshape(GP, SSM_GROUP), (1, N_GROUPS))
    return pl.pallas_call(
        _s5_prep_kernel,
        out_shape=(jax.ShapeDtypeStruct((1, GP), F32), jax.ShapeDtypeStruct((1, GP), F32),
                   jax.ShapeDtypeStruct((D_SSM, 2 * GP), BF16), jax.ShapeDtypeStruct((2 * GP, D_SSM), BF16)),
        compiler_params=pltpu.CompilerParams(vmem_limit_bytes=VMEM_LIMIT),
        name="s5_prep",
    )(lr, li, ls, br, bi, cr, ci)


def _scan_pitch(tt):
    return tt if tt % 16 else tt + V7X_SUBLANES


def _mixer_kernel(nb, tt,
                  x_ref, cos_ref, sin_ref, h0_ref, pre_g, w_in, bb, are_ref, aim_ref, cc, d_skip,
                  w_glu, b_glu, ssm_g, q_g, w_uq, w_uk, kv_g,
                  ys_ref, qlat_ref, qpe_ref, kcat_ref, ckv_ref, kpe_ref, st_ref,
                  hbuf):
    g, rpg, _ = x_ref.shape
    r = nb * tt
    pitch = _scan_pitch(tt)

    @pl.when(pl.program_id(0) == 0)
    def _():
        st_ref[...] = h0_ref[...]

    x = x_ref[...].reshape(r, D_MODEL)
    h = _rms(x, pre_g[...]).astype(BF16)
    z = _dot(h, w_in[...])
    u = z[:, :D_SSM]
    hq = z[:, D_SSM:D_SSM + Q_LORA]
    hkv = z[:, D_SSM + Q_LORA:D_SSM + Q_LORA + KV_LORA]
    hpe = z[:, D_SSM + Q_LORA + KV_LORA:]

    u_bf = u.astype(BF16)
    tiles_per_part = GP // MXU_N
    for j in range(2 * tiles_per_part):
        k0 = ((j % tiles_per_part) // 2) * V7X_LANES
        bu = _dot(u_bf[:, k0:k0 + V7X_LANES], bb[k0:k0 + V7X_LANES, j * MXU_N:(j + 1) * MXU_N])
        for half in range(MXU_N // V7X_LANES):
            c = j * (MXU_N // V7X_LANES) + half
            lanes = slice(half * V7X_LANES, (half + 1) * V7X_LANES)
            if pitch == tt:
                hbuf[c] = bu[:, lanes]
            else:
                for b in range(nb):
                    hbuf[c, b * pitch:b * pitch + tt, :] = bu[b * tt:(b + 1) * tt, lanes]

    n_sub = nb // V7X_SUBLANES
    for lc in range(N_SLABS // SCAN_SLABS):
        slabs = [lc * SCAN_SLABS + k for k in range(SCAN_SLABS)]
        a_re = [are_ref[:, s * V7X_LANES:(s + 1) * V7X_LANES] for s in slabs]
        a_im = [aim_ref[:, s * V7X_LANES:(s + 1) * V7X_LANES] for s in slabs]

        def sub_body(sg, carry, slabs=slabs, a_re=a_re, a_im=a_im):
            seqs = pl.ds(pl.multiple_of(sg * V7X_SUBLANES, V7X_SUBLANES), V7X_SUBLANES)
            row0 = sg * (V7X_SUBLANES * pitch)

            def t_body(t, state):
                rows = pl.ds(row0 + t, V7X_SUBLANES, stride=pitch)
                new = []
                for k, s in enumerate(slabs):
                    s_re, s_im = state[2 * k], state[2 * k + 1]
                    n_re = a_re[k] * s_re - a_im[k] * s_im + hbuf[s, rows, :]
                    n_im = a_re[k] * s_im + a_im[k] * s_re + hbuf[N_SLABS + s, rows, :]
                    hbuf[s, rows, :] = n_re
                    hbuf[N_SLABS + s, rows, :] = n_im
                    new += [n_re, n_im]
                return tuple(new)

            init = []
            for s in slabs:
                init += [st_ref[seqs, s * V7X_LANES:(s + 1) * V7X_LANES],
                         st_ref[seqs, GP + s * V7X_LANES:GP + (s + 1) * V7X_LANES]]
            fin = lax.fori_loop(0, tt, t_body, tuple(init), unroll=min(tt, 4))
            for k, s in enumerate(slabs):
                st_ref[seqs, s * V7X_LANES:(s + 1) * V7X_LANES] = fin[2 * k]
                st_ref[seqs, GP + s * V7X_LANES:GP + (s + 1) * V7X_LANES] = fin[2 * k + 1]
            return carry

        lax.fori_loop(0, n_sub, sub_body, 0)

    def slab_rows(c):
        if pitch == tt:
            return hbuf[c]
        return jnp.concatenate([hbuf[c, b * pitch:b * pitch + tt, :] for b in range(nb)], axis=0)

    y_tiles = []
    k_per_tile = GP * MXU_N // D_SSM
    slabs_per_tile = k_per_tile // V7X_LANES
    for n in range(D_SSM // MXU_N):
        cols = slice(n * MXU_N, (n + 1) * MXU_N)
        acc = None
        for part in range(2):
            s0 = part * N_SLABS + n * slabs_per_tile
            hs = jnp.concatenate([slab_rows(s0 + c) for c in range(slabs_per_tile)], axis=1).astype(BF16)
            k0 = part * GP + n * k_per_tile
            term = _dot(hs, cc[k0:k0 + k_per_tile, cols])
            acc = term if acc is None else acc + term
        y_tiles.append(acc)
    y = jnp.concatenate(y_tiles, axis=1) + d_skip[...] * u
    y = jax.nn.gelu(y, approximate=True)
    y = y * jax.nn.sigmoid(_dot(y.astype(BF16), w_glu[...]) + b_glu[...])
    ys_ref[...] = _rms(y, ssm_g[...]).astype(BF16).reshape(g, rpg, D_SSM)

    cos = cos_ref[...]
    sin = sin_ref[...]

    def rope(x1, x2):
        x1 = x1.reshape(g, rpg, V7X_LANES)
        x2 = x2.reshape(g, rpg, V7X_LANES)
        return x1 * cos - x2 * sin, x2 * cos + x1 * sin

    q = _dot(_rms(hq, q_g[...]).astype(BF16), w_uq[...]) * SOFTMAX_SCALE
    q_nope = q[:, :N_HEADS * QK_NOPE]
    q1, q2 = rope(q[:, N_HEADS * QK_NOPE:N_HEADS * QK_NOPE + V7X_LANES],
                  q[:, N_HEADS * QK_NOPE + V7X_LANES:])
    qpe_ref[:, :, :V7X_LANES] = q1.astype(BF16)
    qpe_ref[:, :, V7X_LANES:] = q2.astype(BF16)
    q_nope = q_nope.astype(BF16)
    for hd in range(N_HEADS):
        k0 = (hd * QK_NOPE // V7X_LANES) * V7X_LANES
        cols = slice(hd * KV_LORA, (hd + 1) * KV_LORA)
        qlat_ref[:, :, cols] = _dot(q_nope[:, k0:k0 + V7X_LANES], w_uk[k0:k0 + V7X_LANES, cols]).astype(
            BF16).reshape(g, rpg, KV_LORA)

    ckv = _rms(hkv, kv_g[...]).reshape(g, rpg, KV_LORA)
    ckv_ref[...] = ckv
    k1, k2 = rope(hpe[:, :V7X_LANES], hpe[:, V7X_LANES:])
    kcat_ref[:, :, :KV_LORA] = ckv.astype(BF16)
    kcat_ref[:, :, KV_LORA:KV_LORA + V7X_LANES] = k1.astype(BF16)
    kcat_ref[:, :, KV_LORA + V7X_LANES:] = k2.astype(BF16)
    kpe_ref[:, :, :HALF_ROPE] = k1[:, :, :HALF_ROPE]
    kpe_ref[:, :, HALF_ROPE:] = k2[:, :, :HALF_ROPE]


def _mixer(x1, cos, sin, h0, w, nb, tt):
    n_g, length, _ = x1.shape
    rpg = nb * tt // n_g
    assert length % rpg == 0 and rpg % 16 == 0 and nb % V7X_SUBLANES == 0
    pitch = _scan_pitch(tt)
    row = lambda width: pl.BlockSpec((n_g, rpg, width), lambda i: (0, i, 0))
    out = lambda width, dtype: jax.ShapeDtypeStruct((n_g, length, width), dtype)
    weight_specs = [
        _const_spec((1, D_MODEL)), _const_spec((D_MODEL, D_IN_PACKED)), _const_spec((D_SSM, 2 * GP)),
        _const_spec((1, GP)), _const_spec((1, GP)), _const_spec((2 * GP, D_SSM)), _const_spec((1, D_SSM)),
        _const_spec((D_SSM, D_SSM)), _const_spec((1, D_SSM)), _const_spec((1, D_SSM)),
        _const_spec((1, Q_LORA)), _const_spec((Q_LORA, D_Q_PACKED)),
        _const_spec((N_HEADS * QK_NOPE, N_HEADS * KV_LORA)), _const_spec((1, KV_LORA)),
    ]
    rope_spec = pl.BlockSpec((rpg, V7X_LANES), lambda i: (i, 0))
    return pl.pallas_call(
        functools.partial(_mixer_kernel, nb, tt),
        grid=(length // rpg,),
        in_specs=[row(D_MODEL), rope_spec, rope_spec, _const_spec((nb, 2 * GP))] + weight_specs,
        out_specs=[row(D_SSM), row(N_HEADS * KV_LORA), row(PE_W), row(KV_LORA + PE_W), row(KV_LORA),
                   row(QK_ROPE), pl.BlockSpec((nb, 2 * GP), lambda i: (0, 0))],
        out_shape=[out(D_SSM, BF16), out(N_HEADS * KV_LORA, BF16), out(PE_W, BF16),
                   out(KV_LORA + PE_W, BF16), out(KV_LORA, F32), out(QK_ROPE, F32),
                   jax.ShapeDtypeStruct((nb, 2 * GP), F32)],
        scratch_shapes=[pltpu.VMEM((2 * N_SLABS, nb * pitch, V7X_LANES), F32)],
        compiler_params=_params(1),
        name="mixer",
    )(x1, cos, sin, h0, *w)


def _prompt_attn_kernel(tq, qlat_ref, qpe_ref, kcat_ref, w_uv, att_g, ya_ref, acc_ref):
    i = pl.program_id(1)
    qpe = qpe_ref[...]
    head_of_lane = lax.shift_right_logical(
        lax.broadcasted_iota(jnp.int32, (tq, PE_W), 1) & (V7X_LANES - 1), 4)
    q_heads = []
    for hd in range(N_HEADS):
        q_heads.append(jnp.concatenate(
            [qlat_ref[:, hd * KV_LORA:(hd + 1) * KV_LORA],
             jnp.where(head_of_lane == hd, qpe, jnp.zeros_like(qpe))], axis=-1))
    hpc = N_HEADS // ATT_ROW_CHUNKS
    rows = hpc * tq
    qs = [jnp.concatenate(q_heads[c * hpc:(c + 1) * hpc], axis=0) for c in range(ATT_ROW_CHUNKS)]

    def block(j, state, masked):
        k = kcat_ref[pl.ds(pl.multiple_of(j * tq, tq), tq), :]
        v = k[:, :KV_LORA]
        scores = [_dot_nt(q, k) for q in qs]
        new_state = []
        for c, s in enumerate(scores):
            m, l = state[2 * c], state[2 * c + 1]
            if masked:
                q_pos = lax.broadcasted_iota(jnp.int32, (rows, tq), 0) & (tq - 1)
                k_pos = lax.broadcasted_iota(jnp.int32, (rows, tq), 1)
                s = jnp.where(k_pos <= q_pos, s, -jnp.inf)
            m_new = jnp.maximum(m, jnp.max(s, axis=-1, keepdims=True))
            alpha = jnp.exp(m - m_new)
            p = jnp.exp(s - m_new)
            l_new = alpha * l + jnp.sum(p, axis=-1, keepdims=True)
            acc_rows = pl.ds(c * rows, rows)
            acc_ref[acc_rows, :] = alpha * acc_ref[acc_rows, :] + _dot(p.astype(BF16), v)
            new_state += [m_new, l_new]
        return tuple(new_state)

    acc_ref[...] = jnp.zeros_like(acc_ref)
    init = (jnp.full((rows, 1), -jnp.inf, F32), jnp.zeros((rows, 1), F32)) * ATT_ROW_CHUNKS
    state = block(i, init, True)
    state = lax.fori_loop(0, i, lambda j, st: block(j, st, False), state)

    l_all = jnp.concatenate([state[2 * c + 1] for c in range(ATT_ROW_CHUNKS)], axis=0)
    o = acc_ref[...] / l_all
    o_cat = jnp.concatenate([o[hd * tq:(hd + 1) * tq, :] for hd in range(N_HEADS)], axis=-1)
    y = _dot(o_cat.astype(BF16), w_uv[...])
    ya_ref[...] = _rms(y, att_g[...]).astype(BF16)


def _prompt_attn(qlat, qpe, kcat, w_uv, att_g):
    bsz, t, _ = qlat.shape
    tq = min(ATT_TQ, t)
    assert t % tq == 0 and tq & (tq - 1) == 0
    return pl.pallas_call(
        functools.partial(_prompt_attn_kernel, tq),
        grid=(bsz, t // tq),
        in_specs=[pl.BlockSpec((None, tq, N_HEADS * KV_LORA), lambda b, i: (b, i, 0)),
                  pl.BlockSpec((None, tq, PE_W), lambda b, i: (b, i, 0)),
                  pl.BlockSpec((None, t, KV_LORA + PE_W), lambda b, i: (b, 0, 0)),
                  _const_spec((N_HEADS * KV_LORA, D_ATT)), _const_spec((1, D_ATT))],
        out_specs=pl.BlockSpec((None, tq, D_ATT), lambda b, i: (b, i, 0)),
        out_shape=jax.ShapeDtypeStruct((bsz, t, D_ATT), BF16),
        scratch_shapes=[pltpu.VMEM((N_HEADS * tq, KV_LORA), F32)],
        compiler_params=_params(2),
        name="prompt_attn",
    )(qlat, qpe, kcat, w_uv, att_g)


def _merge(m, l, acc, m_c, l_c, acc_c):
    m_new = jnp.maximum(m, m_c)
    w = jnp.exp(m - m_new)
    w_c = jnp.exp(m_c - m_new)
    return m_new, w * l + w_c * l_c, w * acc + w_c * acc_c


def _decode_attn_kernel(n_regions, dec_t,
                        pt_ref, qlat_ref, qpe_ref, cnew_ref, penew_ref, ckv_hbm, kpe_hbm,
                        o_ref, kvbuf, pebuf, sem):
    b = pl.program_id(0)
    nb = pl.num_programs(0)
    rows = N_HEADS * dec_t

    def page_copies(bb, region, slot, p):
        page = pt_ref[bb, region * DEC_REGION_PAGES + p]
        return (pltpu.make_async_copy(ckv_hbm.at[0, page], kvbuf.at[slot, p], sem.at[0, slot]),
                pltpu.make_async_copy(kpe_hbm.at[0, page], pebuf.at[slot, p], sem.at[1, slot]))

    def start_region(bb, region, slot):
        for p in range(DEC_REGION_PAGES):
            ckv_cp, kpe_cp = page_copies(bb, region, slot, p)
            ckv_cp.start(priority=p % 2)
            kpe_cp.start(priority=(p + 1) % 2)

    def wait_region(bb, region, slot):
        for p in range(DEC_REGION_PAGES):
            for cp in page_copies(bb, region, slot, p):
                cp.wait()

    @pl.when(b == 0)
    def _():
        start_region(0, 0, 0)

    qlat = qlat_ref[...]
    qpe = qpe_ref[...]
    qlat_f = qlat.astype(F32)
    qpe_f = qpe.astype(F32)

    t_of_row = lax.broadcasted_iota(jnp.int32, (rows, 1), 0) & (dec_t - 1)
    s_new = []
    for j in range(dec_t):
        sj = (jnp.sum(qlat_f * cnew_ref[j:j + 1, :], axis=-1, keepdims=True)
              + jnp.sum(qpe_f * penew_ref[j:j + 1, :], axis=-1, keepdims=True))
        s_new.append(jnp.where(t_of_row >= j, sj, -jnp.inf))
    m = s_new[0]
    for j in range(1, dec_t):
        m = jnp.maximum(m, s_new[j])
    l = jnp.zeros((rows, 1), F32)
    acc = jnp.zeros((rows, KV_LORA), F32)
    for j in range(dec_t):
        pj = jnp.exp(s_new[j] - m)
        l = l + pj
        acc = acc + pj * cnew_ref[j:j + 1, :]

    sub_keys = DEC_SUB_PAGES * V7X_LANES
    for region in range(n_regions):
        slot = region % 2
        if region + 1 < n_regions:
            start_region(b, region + 1, 1 - slot)
        else:
            @pl.when(b + 1 < nb)
            def _():
                start_region(b + 1, 0, 1 - slot)
        wait_region(b, region, slot)
        kvs, scores = [], []
        for sub in range(DEC_REGION_PAGES // DEC_SUB_PAGES):
            p0 = sub * DEC_SUB_PAGES
            kv = kvbuf[slot, p0:p0 + DEC_SUB_PAGES].reshape(sub_keys, KV_LORA).astype(BF16)
            pe_t = jnp.concatenate([pebuf[slot, p0 + p] for p in range(DEC_SUB_PAGES)],
                                   axis=1).astype(BF16)
            kvs.append(kv)
            scores.append(_dot_nt(qlat, kv) + _dot(qpe, pe_t))
        for kv, s in zip(kvs, scores):
            m_c = jnp.max(s, axis=-1, keepdims=True)
            p = jnp.exp(s - m_c)
            l_c = jnp.sum(p, axis=-1, keepdims=True)
            acc_c = _dot(p.astype(BF16), kv)
            m, l, acc = _merge(m, l, acc, m_c, l_c, acc_c)

    o_ref[...] = acc / l


def _decode_attn(page_table, qlat, qpe, cnew, penew, cache_ckv, cache_kpe_t):
    bsz, n_pages = page_table.shape
    dec_t = cnew.shape[1]
    page = cache_ckv.shape[2]
    assert page == V7X_LANES and n_pages % (2 * DEC_REGION_PAGES) == 0 and dec_t & (dec_t - 1) == 0
    rows = N_HEADS * dec_t
    n_regions = n_pages // DEC_REGION_PAGES
    per_seq = lambda r, width: pl.BlockSpec((None, r, width), lambda b, pt: (b, 0, 0))
    grid_spec = pltpu.PrefetchScalarGridSpec(
        num_scalar_prefetch=1,
        grid=(bsz,),
        in_specs=[per_seq(rows, KV_LORA), per_seq(rows, QK_ROPE), per_seq(dec_t, KV_LORA),
                  per_seq(dec_t, QK_ROPE),
                  pl.BlockSpec(memory_space=pl.ANY), pl.BlockSpec(memory_space=pl.ANY)],
        out_specs=per_seq(rows, KV_LORA),
        scratch_shapes=[pltpu.VMEM((2, DEC_REGION_PAGES, page, KV_LORA), F32),
                        pltpu.VMEM((2, DEC_REGION_PAGES, QK_ROPE, page), F32),
                        pltpu.SemaphoreType.DMA((2, 2))],
    )
    return pl.pallas_call(
        functools.partial(_decode_attn_kernel, n_regions, dec_t),
        grid_spec=grid_spec,
        out_shape=jax.ShapeDtypeStruct((bsz, rows, KV_LORA), F32),
        compiler_params=_params(1),
        name="decode_attn",
    )(page_table, qlat, qpe, cnew, penew, cache_ckv, cache_kpe_t)


def _uv_kernel(o_ref, w_uv, att_g, ya_ref):
    y = _dot(o_ref[...].astype(BF16), w_uv[...])
    ya_ref[...] = _rms(y, att_g[...]).astype(BF16)


def _uv_proj(o_cat, w_uv, att_g):
    r = o_cat.shape[0]
    return pl.pallas_call(
        _uv_kernel,
        out_shape=jax.ShapeDtypeStruct((r, D_ATT), BF16),
        compiler_params=pltpu.CompilerParams(vmem_limit_bytes=VMEM_LIMIT),
        name="uv_proj",
    )(o_cat, w_uv, att_g)


def _rope_tables(pos):
    inv = ROPE_THETA ** (-jnp.arange(0, QK_ROPE, 2, dtype=F32) / QK_ROPE)
    ang = pos[:, None] * inv[None, :]
    return jnp.tile(jnp.cos(ang), (1, N_HEADS)), jnp.tile(jnp.sin(ang), (1, N_HEADS))


def kernel(x_prompt, x_sample, cache_ckv, cache_kpe, state_ssm_re, state_ssm_im, page_table, ffn1_pre_g, ffn1_w_gate, ffn1_w_up, ffn1_w_down, ffn1_post_g, mix_pre_g, w_in, ssm_lam_re, ssm_lam_im, ssm_log_step, ssm_b_re, ssm_b_im, ssm_c_re, ssm_c_im, ssm_d, ssm_w_glu, ssm_b_glu, q_norm_g, w_uq, kv_norm_g, w_uk, w_uv, ssm_out_g, att_out_g, w_out, mix_post_g, ffn2_pre_g, ffn2_w_gate, ffn2_w_up, ffn2_w_down, ffn2_post_g):
    assert ffn1_pre_g.shape[0] == 1, "single-layer trunk"
    bsz, seq, _ = x_prompt.shape
    dbs, dseq, _ = x_sample.shape
    past_len = page_table.shape[1] * cache_ckv.shape[2]
    assert bsz == V7X_SUBLANES

    row = lambda v: v[0].reshape(1, -1)
    ffn1_w = (row(ffn1_pre_g), ffn1_w_gate[0].astype(BF16), ffn1_w_up[0].astype(BF16),
              ffn1_w_down[0].astype(BF16), row(ffn1_post_g))
    ffn2_w = (w_out[0, :D_SSM].astype(BF16), w_out[0, D_SSM:].astype(BF16), row(mix_post_g),
              row(ffn2_pre_g), ffn2_w_gate[0].astype(BF16), ffn2_w_up[0].astype(BF16),
              ffn2_w_down[0].astype(BF16), row(ffn2_post_g))
    off_pe = D_SSM + Q_LORA + KV_LORA
    w_in_p = jnp.concatenate(
        [w_in[0, :, :off_pe],
         jnp.tile(w_in[0, :, off_pe:off_pe + HALF_ROPE], (1, N_HEADS)),
         jnp.tile(w_in[0, :, off_pe + HALF_ROPE:], (1, N_HEADS))], axis=1).astype(BF16)
    wq = w_uq[0]
    w_uq_p = jnp.concatenate(
        [wq[:, :, :QK_NOPE].reshape(Q_LORA, N_HEADS * QK_NOPE),
         wq[:, :, QK_NOPE:QK_NOPE + HALF_ROPE].reshape(Q_LORA, N_HEADS * HALF_ROPE),
         wq[:, :, QK_NOPE + HALF_ROPE:].reshape(Q_LORA, N_HEADS * HALF_ROPE)], axis=1).astype(BF16)
    eye_h = jnp.eye(N_HEADS, dtype=F32)
    w_uk_bd = jnp.einsum('hnr,hg->hngr', w_uk[0].transpose(1, 2, 0), eye_h).reshape(
        N_HEADS * QK_NOPE, N_HEADS * KV_LORA).astype(BF16)
    w_uv_bd = jnp.einsum('hrv,hg->hrgv', w_uv[0].transpose(1, 0, 2), eye_h).reshape(
        N_HEADS * KV_LORA, D_ATT).astype(BF16)
    a_re, a_im, bb, cc = _s5_prep(ssm_lam_re[0], ssm_lam_im[0], ssm_log_step[0],
                                  ssm_b_re[0], ssm_b_im[0], ssm_c_re[0], ssm_c_im[0])
    mix_w = (row(mix_pre_g), w_in_p, bb, a_re, a_im, cc, row(ssm_d), ssm_w_glu[0].astype(BF16),
             row(ssm_b_glu), row(ssm_out_g), row(q_norm_g), w_uq_p, w_uk_bd, row(kv_norm_g))
    att_g = row(att_out_g)

    cos_p, sin_p = _rope_tables(jnp.arange(seq, dtype=F32))
    x1_p = _ffn1(x_prompt.reshape(bsz * seq, D_MODEL), ffn1_w)
    ys_p, qlat_p, qpe_p, kcat_p, ckv_p, kpe_p, st_p = _mixer(
        x1_p.reshape(bsz, seq, D_MODEL), cos_p, sin_p, jnp.zeros((bsz, 2 * GP), F32), mix_w,
        bsz, min(MIX_TT, seq))
    ya_p = _prompt_attn(qlat_p, qpe_p, kcat_p, w_uv_bd, att_g)
    y_p = _ffn2(x1_p, ys_p.reshape(bsz * seq, D_SSM), ya_p.reshape(bsz * seq, D_ATT), ffn2_w)

    n_s = dbs * dseq
    cos_s, sin_s = _rope_tables(past_len + jnp.arange(dseq, dtype=F32))
    x1_s = _ffn1(x_sample.reshape(n_s, D_MODEL), ffn1_w)
    h0_s = jnp.concatenate([state_ssm_re[0].reshape(dbs, GP), state_ssm_im[0].reshape(dbs, GP)], axis=1)
    ys_s, qlat_s, qpe_s, _, ckv_s, kpe_s, st_s = _mixer(
        x1_s.reshape(1, n_s, D_MODEL), jnp.tile(cos_s, (dbs, 1)), jnp.tile(sin_s, (dbs, 1)), h0_s, mix_w,
        dbs, dseq)
    qlat_d = qlat_s.reshape(dbs, dseq, N_HEADS, KV_LORA).transpose(0, 2, 1, 3).reshape(
        dbs, N_HEADS * dseq, KV_LORA)
    qpe_d = qpe_s.reshape(dbs, dseq, 2, N_HEADS, HALF_ROPE).transpose(0, 3, 1, 2, 4).reshape(
        dbs, N_HEADS * dseq, QK_ROPE)
    cnew = ckv_s.reshape(dbs, dseq, KV_LORA)
    penew = kpe_s.reshape(dbs, dseq, QK_ROPE)
    o_d = _decode_attn(page_table, qlat_d, qpe_d, cnew, penew, cache_ckv, cache_kpe.transpose(0, 1, 3, 2))
    o_cat = o_d.reshape(dbs, N_HEADS, dseq, KV_LORA).transpose(0, 2, 1, 3).reshape(n_s, N_HEADS * KV_LORA)
    ya_s = _uv_proj(o_cat, w_uv_bd, att_g)
    y_s = _ffn2(x1_s, ys_s.reshape(n_s, D_SSM), ya_s, ffn2_w)

    state = lambda st, n: (st[:, :GP].reshape(1, n, N_GROUPS, SSM_STATE),
                           st[:, GP:].reshape(1, n, N_GROUPS, SSM_STATE))
    re_p, im_p = state(st_p, bsz)
    re_s, im_s = state(st_s, dbs)
    return (y_p.reshape(bsz, seq, D_MODEL), y_s.reshape(dbs, dseq, D_MODEL),
            ckv_p[None], kpe_p[None], re_p, im_p,
            cnew[None], penew[None], re_s, im_s)
```

```python
import functools
import math

import jax
import jax.numpy as jnp
from jax import lax
from jax.experimental import pallas as pl
from jax.experimental.pallas import tpu as pltpu

F32 = jnp.float32
BF16 = jnp.bfloat16

D_MODEL = 1024
D_FF = 2816
EPS = 1e-6
D_SSM = 512
SSM_GROUP = 16
N_GROUPS = D_SSM // SSM_GROUP
SSM_STATE = 64
GP = N_GROUPS * SSM_STATE
N_HEADS = 8
QK_NOPE = 64
QK_ROPE = 32
HALF_ROPE = QK_ROPE // 2
V_DIM = 64
D_ATT = N_HEADS * V_DIM
Q_LORA = 256
KV_LORA = 256
ROPE_THETA = 10000.0
SOFTMAX_SCALE = 1.0 / math.sqrt(QK_NOPE + QK_ROPE)
D_IN_PACKED = D_SSM + Q_LORA + KV_LORA + 2 * N_HEADS * HALF_ROPE
D_Q_PACKED = N_HEADS * QK_NOPE + 2 * N_HEADS * HALF_ROPE
PE_W = 2 * N_HEADS * HALF_ROPE

V7X_SUBLANES = 8
V7X_LANES = 128
MXU_N = 256
VMEM_LIMIT = 56 * 1024 * 1024

FFN_TM = 512
FFN_SUB_TILES = 2
MIX_TT = 64
ATT_TQ = 256
ATT_ROW_CHUNKS = 4
DEC_REGION_PAGES = 64
DEC_SEQS_PER_STEP = 2
DEC_AHEAD = 2
DEC_SUB_PAGES = 16
SCAN_SLABS = 4
N_SLABS = GP // V7X_LANES


def _const_spec(shape):
    nd = len(shape)
    return pl.BlockSpec(shape, lambda *_: (0,) * nd, pipeline_mode=pl.Buffered(1))


def _rms(x, g):
    return x * lax.rsqrt(jnp.mean(x * x, axis=-1, keepdims=True) + EPS) * g


def _dot(a, b):
    return jnp.dot(a, b, preferred_element_type=F32)


def _dot_nt(a, b):
    return lax.dot_general(a, b, (((1,), (1,)), ((), ())), preferred_element_type=F32)


def _ffn(xs, pre_g, wg_ref, wu_ref, wd_ref, post_g):
    hs = [_rms(x, pre_g).astype(BF16) for x in xs]
    gates = [_dot(h, wg_ref[...]) for h in hs]
    ups = [_dot(h, wu_ref[...]) for h in hs]
    acts = [(gate * jax.nn.sigmoid(gate) * up).astype(BF16) for gate, up in zip(gates, ups)]
    fs = [_dot(act, wd_ref[...]) for act in acts]
    return [x + 0.5 * _rms(f, post_g) for x, f in zip(xs, fs)]


def _params(n_grid_axes):
    return pltpu.CompilerParams(dimension_semantics=("arbitrary",) * n_grid_axes,
                                vmem_limit_bytes=VMEM_LIMIT)


def _sub_tiles(n_rows):
    sub = n_rows // FFN_SUB_TILES if n_rows % (FFN_SUB_TILES * 16) == 0 else n_rows
    return [pl.ds(r0, sub) for r0 in range(0, n_rows, sub)]


def _ffn1_kernel(x_ref, pre_g, wg, wu, wd, post_g, o_ref):
    tiles = _sub_tiles(x_ref.shape[0])
    outs = _ffn([x_ref[rows, :] for rows in tiles], pre_g[...], wg, wu, wd, post_g[...])
    for rows, out in zip(tiles, outs):
        o_ref[rows, :] = out


def _ffn_weight_specs():
    return [_const_spec((1, D_MODEL)), _const_spec((D_MODEL, D_FF)), _const_spec((D_MODEL, D_FF)),
            _const_spec((D_FF, D_MODEL)), _const_spec((1, D_MODEL))]


def _ffn1(x, w):
    r = x.shape[0]
    tm = min(FFN_TM, r)
    rows = pl.BlockSpec((tm, D_MODEL), lambda i: (i, 0))
    return pl.pallas_call(
        _ffn1_kernel,
        grid=(r // tm,),
        in_specs=[rows] + _ffn_weight_specs(),
        out_specs=rows,
        out_shape=jax.ShapeDtypeStruct((r, D_MODEL), F32),
        compiler_params=_params(1),
        name="ffn1",
    )(x, *w)


def _ffn2_kernel(x_ref, ys_ref, ya_ref, wo_s, wo_a, mix_g, pre_g, wg, wu, wd, post_g, o_ref):
    tiles = _sub_tiles(x_ref.shape[0])
    xs = []
    for rows in tiles:
        mixed = _dot(ys_ref[rows, :], wo_s[...]) + _dot(ya_ref[rows, :], wo_a[...])
        xs.append(x_ref[rows, :] + _rms(mixed, mix_g[...]))
    outs = _ffn(xs, pre_g[...], wg, wu, wd, post_g[...])
    for rows, out in zip(tiles, outs):
        o_ref[rows, :] = out


def _ffn2(x1, ys, ya, w):
    r = x1.shape[0]
    tm = min(FFN_TM, r)
    rows = lambda width: pl.BlockSpec((tm, width), lambda i: (i, 0))
    weight_specs = [_const_spec((D_SSM, D_MODEL)), _const_spec((D_ATT, D_MODEL)),
                    _const_spec((1, D_MODEL))] + _ffn_weight_specs()
    return pl.pallas_call(
        _ffn2_kernel,
        grid=(r // tm,),
        in_specs=[rows(D_MODEL), rows(D_SSM), rows(D_ATT)] + weight_specs,
        out_specs=rows(D_MODEL),
        out_shape=jax.ShapeDtypeStruct((r, D_MODEL), F32),
        compiler_params=_params(1),
        name="ffn2",
    )(x1, ys, ya, *w)


def _s5_prep_kernel(lr_ref, li_ref, ls_ref, br_ref, bi_ref, cr_ref, ci_ref,
                    are_ref, aim_ref, bb_ref, cc_ref):
    lr = lr_ref[...]
    li = li_ref[...]
    dt = jnp.exp(ls_ref[...])
    mag = jnp.exp(lr * dt)
    a_re = mag * jnp.cos(li * dt)
    a_im = mag * jnp.sin(li * dt)
    den = lr * lr + li * li
    n_re = a_re - 1.0
    z_re = (n_re * lr + a_im * li) / den
    z_im = (a_im * lr - n_re * li) / den
    are_ref[...] = a_re
    aim_ref[...] = a_im

    row_g = lax.shift_right_logical(lax.broadcasted_iota(jnp.int32, (D_SSM, GP), 0), 4)
    col_g = lax.shift_right_logical(lax.broadcasted_iota(jnp.int32, (D_SSM, GP), 1), 6)
    same = row_g == col_g
    br = br_ref[...]
    bi = bi_ref[...]
    bb_ref[:, :GP] = jnp.where(same, z_re * br - z_im * bi, 0.0).astype(BF16)
    bb_ref[:, GP:] = jnp.where(same, z_re * bi + z_im * br, 0.0).astype(BF16)

    row_g = lax.shift_right_logical(lax.broadcasted_iota(jnp.int32, (GP, D_SSM), 0), 6)
    col_g = lax.shift_right_logical(lax.broadcasted_iota(jnp.int32, (GP, D_SSM), 1), 4)
    same = row_g == col_g
    cc_ref[:GP, :] = jnp.where(same, cr_ref[...], 0.0).astype(BF16)
    cc_ref[GP:, :] = jnp.where(same, -ci_ref[...], 0.0).astype(BF16)


def _s5_prep(lam_re, lam_im, log_step, b_re, b_im, c_re, c_im):
    lr = lam_re.reshape(1, GP)
    li = lam_im.reshape(1, GP)
    ls = jnp.repeat(log_step, SSM_STATE).reshape(1, GP)
    br = jnp.tile(b_re.transpose(2, 0, 1).reshape(SSM_GROUP, GP), (N_GROUPS, 1))
    bi = jnp.tile(b_im.transpose(2, 0, 1).reshape(SSM_GROUP, GP), (N_GROUPS, 1))
    cr = jnp.tile(c_re.transpose(0, 2, 1).reshape(GP, SSM_GROUP), (1, N_GROUPS))
    ci = jnp.tile(c_im.transpose(0, 2, 1).reshape(GP, SSM_GROUP), (1, N_GROUPS))
    return pl.pallas_call(
        _s5_prep_kernel,
        out_shape=(jax.ShapeDtypeStruct((1, GP), F32), jax.ShapeDtypeStruct((1, GP), F32),
                   jax.ShapeDtypeStruct((D_SSM, 2 * GP), BF16), jax.ShapeDtypeStruct((2 * GP, D_SSM), BF16)),
        compiler_params=pltpu.CompilerParams(vmem_limit_bytes=VMEM_LIMIT),
        name="s5_prep",
    )(lr, li, ls, br, bi, cr, ci)


def _scan_pitch(tt):
    return tt if tt % 16 else tt + V7X_SUBLANES


def _mixer_kernel(nb, tt,
                  x_ref, cos_ref, sin_ref, h0_ref, pre_g, w_in, bb, are_ref, aim_ref, cc, d_skip,
                  w_glu, b_glu, ssm_g, q_g, w_uq, w_uk, kv_g,
                  ys_ref, qlat_ref, qpe_ref, kcat_ref, ckv_ref, kpe_ref, st_ref,
                  hbuf):
    g, rpg, _ = x_ref.shape
    r = nb * tt
    pitch = _scan_pitch(tt)

    @pl.when(pl.program_id(0) == 0)
    def _():
        st_ref[...] = h0_ref[...]

    x = x_ref[...].reshape(r, D_MODEL)
    h = _rms(x, pre_g[...]).astype(BF16)
    z = _dot(h, w_in[...])
    u = z[:, :D_SSM]
    hq = z[:, D_SSM:D_SSM + Q_LORA]
    hkv = z[:, D_SSM + Q_LORA:D_SSM + Q_LORA + KV_LORA]
    hpe = z[:, D_SSM + Q_LORA + KV_LORA:]

    u_bf = u.astype(BF16)
    tiles_per_part = GP // MXU_N
    for j in range(2 * tiles_per_part):
        k0 = ((j % tiles_per_part) // 2) * V7X_LANES
        bu = _dot(u_bf[:, k0:k0 + V7X_LANES], bb[k0:k0 + V7X_LANES, j * MXU_N:(j + 1) * MXU_N])
        for half in range(MXU_N // V7X_LANES):
            c = j * (MXU_N // V7X_LANES) + half
            lanes = slice(half * V7X_LANES, (half + 1) * V7X_LANES)
            if pitch == tt:
                hbuf[c] = bu[:, lanes]
            else:
                for b in range(nb):
                    hbuf[c, b * pitch:b * pitch + tt, :] = bu[b * tt:(b + 1) * tt, lanes]

    n_sub = nb // V7X_SUBLANES
    for lc in range(N_SLABS // SCAN_SLABS):
        slabs = [lc * SCAN_SLABS + k for k in range(SCAN_SLABS)]
        a_re = [are_ref[:, s * V7X_LANES:(s + 1) * V7X_LANES] for s in slabs]
        a_im = [aim_ref[:, s * V7X_LANES:(s + 1) * V7X_LANES] for s in slabs]

        def sub_body(sg, carry, slabs=slabs, a_re=a_re, a_im=a_im):
            seqs = pl.ds(pl.multiple_of(sg * V7X_SUBLANES, V7X_SUBLANES), V7X_SUBLANES)
            row0 = sg * (V7X_SUBLANES * pitch)

            def t_body(t, state):
                rows = pl.ds(row0 + t, V7X_SUBLANES, stride=pitch)
                new = []
                for k, s in enumerate(slabs):
                    s_re, s_im = state[2 * k], state[2 * k + 1]
                    n_re = a_re[k] * s_re - a_im[k] * s_im + hbuf[s, rows, :]
                    n_im = a_re[k] * s_im + a_im[k] * s_re + hbuf[N_SLABS + s, rows, :]
                    hbuf[s, rows, :] = n_re
                    hbuf[N_SLABS + s, rows, :] = n_im
                    new += [n_re, n_im]
                return tuple(new)

            init = []
            for s in slabs:
                init += [st_ref[seqs, s * V7X_LANES:(s + 1) * V7X_LANES],
                         st_ref[seqs, GP + s * V7X_LANES:GP + (s + 1) * V7X_LANES]]
            fin = lax.fori_loop(0, tt, t_body, tuple(init), unroll=min(tt, 4))
            for k, s in enumerate(slabs):
                st_ref[seqs, s * V7X_LANES:(s + 1) * V7X_LANES] = fin[2 * k]
                st_ref[seqs, GP + s * V7X_LANES:GP + (s + 1) * V7X_LANES] = fin[2 * k + 1]
            return carry

        lax.fori_loop(0, n_sub, sub_body, 0)

    def slab_rows(c):
        if pitch == tt:
            return hbuf[c]
        return jnp.concatenate([hbuf[c, b * pitch:b * pitch + tt, :] for b in range(nb)], axis=0)

    y_tiles = []
    k_per_tile = GP * MXU_N // D_SSM
    slabs_per_tile = k_per_tile // V7X_LANES
    for n in range(D_SSM // MXU_N):
        cols = slice(n * MXU_N, (n + 1) * MXU_N)
        acc = None
        for part in range(2):
            s0 = part * N_SLABS + n * slabs_per_tile
            hs = jnp.concatenate([slab_rows(s0 + c) for c in range(slabs_per_tile)], axis=1).astype(BF16)
            k0 = part * GP + n * k_per_tile
            term = _dot(hs, cc[k0:k0 + k_per_tile, cols])
            acc = term if acc is None else acc + term
        y_tiles.append(acc)
    y = jnp.concatenate(y_tiles, axis=1) + d_skip[...] * u
    y = jax.nn.gelu(y, approximate=True)
    y = y * jax.nn.sigmoid(_dot(y.astype(BF16), w_glu[...]) + b_glu[...])
    ys_ref[...] = _rms(y, ssm_g[...]).astype(BF16).reshape(g, rpg, D_SSM)

    cos = cos_ref[...]
    sin = sin_ref[...]

    def rope(x1, x2):
        x1 = x1.reshape(g, rpg, V7X_LANES)
        x2 = x2.reshape(g, rpg, V7X_LANES)
        return x1 * cos - x2 * sin, x2 * cos + x1 * sin

    q = _dot(_rms(hq, q_g[...]).astype(BF16), w_uq[...]) * SOFTMAX_SCALE
    q_nope = q[:, :N_HEADS * QK_NOPE]
    q1, q2 = rope(q[:, N_HEADS * QK_NOPE:N_HEADS * QK_NOPE + V7X_LANES],
                  q[:, N_HEADS * QK_NOPE + V7X_LANES:])
    qpe_ref[:, :, :V7X_LANES] = q1.astype(BF16)
    qpe_ref[:, :, V7X_LANES:] = q2.astype(BF16)
    q_nope = q_nope.astype(BF16)
    for hd in range(N_HEADS):
        k0 = (hd * QK_NOPE // V7X_LANES) * V7X_LANES
        cols = slice(hd * KV_LORA, (hd + 1) * KV_LORA)
        qlat_ref[:, :, cols] = _dot(q_nope[:, k0:k0 + V7X_LANES], w_uk[k0:k0 + V7X_LANES, cols]).astype(
            BF16).reshape(g, rpg, KV_LORA)

    ckv = _rms(hkv, kv_g[...]).reshape(g, rpg, KV_LORA)
    ckv_ref[...] = ckv
    k1, k2 = rope(hpe[:, :V7X_LANES], hpe[:, V7X_LANES:])
    kcat_ref[:, :, :KV_LORA] = ckv.astype(BF16)
    kcat_ref[:, :, KV_LORA:KV_LORA + V7X_LANES] = k1.astype(BF16)
    kcat_ref[:, :, KV_LORA + V7X_LANES:] = k2.astype(BF16)
    kpe_ref[:, :, :HALF_ROPE] = k1[:, :, :HALF_ROPE]
    kpe_ref[:, :, HALF_ROPE:] = k2[:, :, :HALF_ROPE]


def _mixer(x1, cos, sin, h0, w, nb, tt):
    n_g, length, _ = x1.shape
    rpg = nb * tt // n_g
    assert length % rpg == 0 and rpg % 16 == 0 and nb % V7X_SUBLANES == 0
    pitch = _scan_pitch(tt)
    row = lambda width: pl.BlockSpec((n_g, rpg, width), lambda i: (0, i, 0))
    out = lambda width, dtype: jax.ShapeDtypeStruct((n_g, length, width), dtype)
    weight_specs = [
        _const_spec((1, D_MODEL)), _const_spec((D_MODEL, D_IN_PACKED)), _const_spec((D_SSM, 2 * GP)),
        _const_spec((1, GP)), _const_spec((1, GP)), _const_spec((2 * GP, D_SSM)), _const_spec((1, D_SSM)),
        _const_spec((D_SSM, D_SSM)), _const_spec((1, D_SSM)), _const_spec((1, D_SSM)),
        _const_spec((1, Q_LORA)), _const_spec((Q_LORA, D_Q_PACKED)),
        _const_spec((N_HEADS * QK_NOPE, N_HEADS * KV_LORA)), _const_spec((1, KV_LORA)),
    ]
    rope_spec = pl.BlockSpec((rpg, V7X_LANES), lambda i: (i, 0))
    return pl.pallas_call(
        functools.partial(_mixer_kernel, nb, tt),
        grid=(length // rpg,),
        in_specs=[row(D_MODEL), rope_spec, rope_spec, _const_spec((nb, 2 * GP))] + weight_specs,
        out_specs=[row(D_SSM), row(N_HEADS * KV_LORA), row(PE_W), row(KV_LORA + PE_W), row(KV_LORA),
                   row(QK_ROPE), pl.BlockSpec((nb, 2 * GP), lambda i: (0, 0))],
        out_shape=[out(D_SSM, BF16), out(N_HEADS * KV_LORA, BF16), out(PE_W, BF16),
                   out(KV_LORA + PE_W, BF16), out(KV_LORA, F32), out(QK_ROPE, F32),
                   jax.ShapeDtypeStruct((nb, 2 * GP), F32)],
        scratch_shapes=[pltpu.VMEM((2 * N_SLABS, nb * pitch, V7X_LANES), F32)],
        compiler_params=_params(1),
        name="mixer",
    )(x1, cos, sin, h0, *w)


def _prompt_attn_kernel(tq, qlat_ref, qpe_ref, kcat_ref, w_uv, att_g, ya_ref, acc_ref):
    i = pl.program_id(1)
    qpe = qpe_ref[...]
    head_of_lane = lax.shift_right_logical(
        lax.broadcasted_iota(jnp.int32, (tq, PE_W), 1) & (V7X_LANES - 1), 4)
    q_heads = []
    for hd in range(N_HEADS):
        q_heads.append(jnp.concatenate(
            [qlat_ref[:, hd * KV_LORA:(hd + 1) * KV_LORA],
             jnp.where(head_of_lane == hd, qpe, jnp.zeros_like(qpe))], axis=-1))
    hpc = N_HEADS // ATT_ROW_CHUNKS
    rows = hpc * tq
    qs = [jnp.concatenate(q_heads[c * hpc:(c + 1) * hpc], axis=0) for c in range(ATT_ROW_CHUNKS)]

    def block(j, state, masked):
        k = kcat_ref[pl.ds(pl.multiple_of(j * tq, tq), tq), :]
        v = k[:, :KV_LORA]
        scores = [_dot_nt(q, k) for q in qs]
        new_state = []
        for c, s in enumerate(scores):
            m, l = state[2 * c], state[2 * c + 1]
            if masked:
                q_pos = lax.broadcasted_iota(jnp.int32, (rows, tq), 0) & (tq - 1)
                k_pos = lax.broadcasted_iota(jnp.int32, (rows, tq), 1)
                s = jnp.where(k_pos <= q_pos, s, -jnp.inf)
            m_new = jnp.maximum(m, jnp.max(s, axis=-1, keepdims=True))
            alpha = jnp.exp(m - m_new)
            p = jnp.exp(s - m_new)
            l_new = alpha * l + jnp.sum(p, axis=-1, keepdims=True)
            acc_rows = pl.ds(c * rows, rows)
            acc_ref[acc_rows, :] = alpha * acc_ref[acc_rows, :] + _dot(p.astype(BF16), v)
            new_state += [m_new, l_new]
        return tuple(new_state)

    acc_ref[...] = jnp.zeros_like(acc_ref)
    init = (jnp.full((rows, 1), -jnp.inf, F32), jnp.zeros((rows, 1), F32)) * ATT_ROW_CHUNKS
    state = block(i, init, True)
    state = lax.fori_loop(0, i, lambda j, st: block(j, st, False), state)

    l_all = jnp.concatenate([state[2 * c + 1] for c in range(ATT_ROW_CHUNKS)], axis=0)
    o = acc_ref[...] / l_all
    o_cat = jnp.concatenate([o[hd * tq:(hd + 1) * tq, :] for hd in range(N_HEADS)], axis=-1)
    y = _dot(o_cat.astype(BF16), w_uv[...])
    ya_ref[...] = _rms(y, att_g[...]).astype(BF16)


def _prompt_attn(qlat, qpe, kcat, w_uv, att_g):
    bsz, t, _ = qlat.shape
    tq = min(ATT_TQ, t)
    assert t % tq == 0 and tq & (tq - 1) == 0
    return pl.pallas_call(
        functools.partial(_prompt_attn_kernel, tq),
        grid=(bsz, t // tq),
        in_specs=[pl.BlockSpec((None, tq, N_HEADS * KV_LORA), lambda b, i: (b, i, 0)),
                  pl.BlockSpec((None, tq, PE_W), lambda b, i: (b, i, 0)),
                  pl.BlockSpec((None, t, KV_LORA + PE_W), lambda b, i: (b, 0, 0)),
                  _const_spec((N_HEADS * KV_LORA, D_ATT)), _const_spec((1, D_ATT))],
        out_specs=pl.BlockSpec((None, tq, D_ATT), lambda b, i: (b, i, 0)),
        out_shape=jax.ShapeDtypeStruct((bsz, t, D_ATT), BF16),
        scratch_shapes=[pltpu.VMEM((N_HEADS * tq, KV_LORA), F32)],
        compiler_params=_params(2),
        name="prompt_attn",
    )(qlat, qpe, kcat, w_uv, att_g)


def _merge(m, l, acc, m_c, l_c, acc_c):
    m_new = jnp.maximum(m, m_c)
    w = jnp.exp(m - m_new)
    w_c = jnp.exp(m_c - m_new)
    return m_new, w * l + w_c * l_c, w * acc + w_c * acc_c


def _new_token_state(qlat, qpe, cnew_ref, penew_ref, seq, dec_t):
    rows = qlat.shape[0]
    qlat_f = qlat.astype(F32)
    qpe_f = qpe.astype(F32)
    t_of_row = lax.broadcasted_iota(jnp.int32, (rows, 1), 0) & (dec_t - 1)
    s_new = []
    for j in range(dec_t):
        sj = (jnp.sum(qlat_f * cnew_ref[seq, j:j + 1, :], axis=-1, keepdims=True)
              + jnp.sum(qpe_f * penew_ref[seq, j:j + 1, :], axis=-1, keepdims=True))
        s_new.append(jnp.where(t_of_row >= j, sj, -jnp.inf))
    m = s_new[0]
    for j in range(1, dec_t):
        m = jnp.maximum(m, s_new[j])
    l = jnp.zeros((rows, 1), F32)
    acc = jnp.zeros((rows, KV_LORA), F32)
    for j in range(dec_t):
        pj = jnp.exp(s_new[j] - m)
        l = l + pj
        acc = acc + pj * cnew_ref[seq, j:j + 1, :]
    return m, l, acc


def _decode_attn_kernel(n_regions, dec_t,
                        pt_ref, qlat_ref, qpe_ref, cnew_ref, penew_ref, ckv_hbm, kpe_hbm,
                        o_ref, kvbuf, pebuf, sem):
    step = pl.program_id(0)
    n_steps = pl.num_programs(0)
    n_slots = DEC_SEQS_PER_STEP * n_regions

    def page_copies(st, slot, p):
        seq = st * DEC_SEQS_PER_STEP + slot // n_regions
        page = pt_ref[seq, (slot % n_regions) * DEC_REGION_PAGES + p]
        return (pltpu.make_async_copy(ckv_hbm.at[0, page], kvbuf.at[slot, p], sem.at[0, slot]),
                pltpu.make_async_copy(kpe_hbm.at[0, page], pebuf.at[slot, p], sem.at[1, slot]))

    def start_region(st, slot):
        for p in range(DEC_REGION_PAGES):
            for cp in page_copies(st, slot, p):
                cp.start()

    def wait_region(st, slot):
        for p in range(DEC_REGION_PAGES):
            for cp in page_copies(st, slot, p):
                cp.wait()

    @pl.when(step == 0)
    def _():
        for slot in range(DEC_AHEAD):
            start_region(0, slot)

    sub_keys = DEC_SUB_PAGES * V7X_LANES
    for slot in range(n_slots):
        seq = slot // n_regions
        if slot % n_regions == 0:
            qlat = qlat_ref[seq]
            qpe = qpe_ref[seq]
            m, l, acc = _new_token_state(qlat, qpe, cnew_ref, penew_ref, seq, dec_t)
        ahead = slot + DEC_AHEAD
        if ahead < n_slots:
            start_region(step, ahead)
        else:
            @pl.when(step + 1 < n_steps)
            def _(ahead=ahead):
                start_region(step + 1, ahead - n_slots)
        wait_region(step, slot)
        kvs, scores = [], []
        for sub in range(DEC_REGION_PAGES // DEC_SUB_PAGES):
            p0 = sub * DEC_SUB_PAGES
            kv = kvbuf[slot, p0:p0 + DEC_SUB_PAGES].reshape(sub_keys, KV_LORA).astype(BF16)
            pe_t = jnp.concatenate([pebuf[slot, p0 + p] for p in range(DEC_SUB_PAGES)],
                                   axis=1).astype(BF16)
            kvs.append(kv)
            scores.append(_dot_nt(qlat, kv) + _dot(qpe, pe_t))
        for kv, s in zip(kvs, scores):
            m_c = jnp.max(s, axis=-1, keepdims=True)
            p = jnp.exp(s - m_c)
            l_c = jnp.sum(p, axis=-1, keepdims=True)
            acc_c = _dot(p.astype(BF16), kv)
            m, l, acc = _merge(m, l, acc, m_c, l_c, acc_c)
        if slot % n_regions == n_regions - 1:
            o_ref[seq] = acc / l


def _decode_attn(page_table, qlat, qpe, cnew, penew, cache_ckv, cache_kpe_t):
    bsz, n_pages = page_table.shape
    dec_t = cnew.shape[1]
    page = cache_ckv.shape[2]
    assert page == V7X_LANES and n_pages % DEC_REGION_PAGES == 0 and dec_t & (dec_t - 1) == 0
    assert bsz % DEC_SEQS_PER_STEP == 0
    rows = N_HEADS * dec_t
    n_regions = n_pages // DEC_REGION_PAGES
    n_slots = DEC_SEQS_PER_STEP * n_regions
    assert n_slots > DEC_AHEAD, "a slot must not be refilled while it is still being read"
    per_step = lambda r, width: pl.BlockSpec((DEC_SEQS_PER_STEP, r, width), lambda s, pt: (s, 0, 0))
    grid_spec = pltpu.PrefetchScalarGridSpec(
        num_scalar_prefetch=1,
        grid=(bsz // DEC_SEQS_PER_STEP,),
        in_specs=[per_step(rows, KV_LORA), per_step(rows, QK_ROPE), per_step(dec_t, KV_LORA),
                  per_step(dec_t, QK_ROPE),
                  pl.BlockSpec(memory_space=pl.ANY), pl.BlockSpec(memory_space=pl.ANY)],
        out_specs=per_step(rows, KV_LORA),
        scratch_shapes=[pltpu.VMEM((n_slots, DEC_REGION_PAGES, page, KV_LORA), F32),
                        pltpu.VMEM((n_slots, DEC_REGION_PAGES, QK_ROPE, page), F32),
                        pltpu.SemaphoreType.DMA((2, n_slots))],
    )
    return pl.pallas_call(
        functools.partial(_decode_attn_kernel, n_regions, dec_t),
        grid_spec=grid_spec,
        out_shape=jax.ShapeDtypeStruct((bsz, rows, KV_LORA), F32),
        compiler_params=_params(1),
        name="decode_attn",
    )(page_table, qlat, qpe, cnew, penew, cache_ckv, cache_kpe_t)


def _uv_kernel(o_ref, w_uv, att_g, ya_ref):
    y = _dot(o_ref[...].astype(BF16), w_uv[...])
    ya_ref[...] = _rms(y, att_g[...]).astype(BF16)


def _uv_proj(o_cat, w_uv, att_g):
    r = o_cat.shape[0]
    return pl.pallas_call(
        _uv_kernel,
        out_shape=jax.ShapeDtypeStruct((r, D_ATT), BF16),
        compiler_params=pltpu.CompilerParams(vmem_limit_bytes=VMEM_LIMIT),
        name="uv_proj",
    )(o_cat, w_uv, att_g)


def _rope_tables(pos):
    inv = ROPE_THETA ** (-jnp.arange(0, QK_ROPE, 2, dtype=F32) / QK_ROPE)
    ang = pos[:, None] * inv[None, :]
    return jnp.tile(jnp.cos(ang), (1, N_HEADS)), jnp.tile(jnp.sin(ang), (1, N_HEADS))


def kernel(x_prompt, x_sample, cache_ckv, cache_kpe, state_ssm_re, state_ssm_im, page_table, ffn1_pre_g, ffn1_w_gate, ffn1_w_up, ffn1_w_down, ffn1_post_g, mix_pre_g, w_in, ssm_lam_re, ssm_lam_im, ssm_log_step, ssm_b_re, ssm_b_im, ssm_c_re, ssm_c_im, ssm_d, ssm_w_glu, ssm_b_glu, q_norm_g, w_uq, kv_norm_g, w_uk, w_uv, ssm_out_g, att_out_g, w_out, mix_post_g, ffn2_pre_g, ffn2_w_gate, ffn2_w_up, ffn2_w_down, ffn2_post_g):
    assert ffn1_pre_g.shape[0] == 1, "single-layer trunk"
    bsz, seq, _ = x_prompt.shape
    dbs, dseq, _ = x_sample.shape
    past_len = page_table.shape[1] * cache_ckv.shape[2]
    assert bsz == V7X_SUBLANES

    row = lambda v: v[0].reshape(1, -1)
    ffn1_w = (row(ffn1_pre_g), ffn1_w_gate[0].astype(BF16), ffn1_w_up[0].astype(BF16),
              ffn1_w_down[0].astype(BF16), row(ffn1_post_g))
    ffn2_w = (w_out[0, :D_SSM].astype(BF16), w_out[0, D_SSM:].astype(BF16), row(mix_post_g),
              row(ffn2_pre_g), ffn2_w_gate[0].astype(BF16), ffn2_w_up[0].astype(BF16),
              ffn2_w_down[0].astype(BF16), row(ffn2_post_g))
    off_pe = D_SSM + Q_LORA + KV_LORA
    w_in_p = jnp.concatenate(
        [w_in[0, :, :off_pe],
         jnp.tile(w_in[0, :, off_pe:off_pe + HALF_ROPE], (1, N_HEADS)),
         jnp.tile(w_in[0, :, off_pe + HALF_ROPE:], (1, N_HEADS))], axis=1).astype(BF16)
    wq = w_uq[0]
    w_uq_p = jnp.concatenate(
        [wq[:, :, :QK_NOPE].reshape(Q_LORA, N_HEADS * QK_NOPE),
         wq[:, :, QK_NOPE:QK_NOPE + HALF_ROPE].reshape(Q_LORA, N_HEADS * HALF_ROPE),
         wq[:, :, QK_NOPE + HALF_ROPE:].reshape(Q_LORA, N_HEADS * HALF_ROPE)], axis=1).astype(BF16)
    eye_h = jnp.eye(N_HEADS, dtype=F32)
    w_uk_bd = jnp.einsum('hnr,hg->hngr', w_uk[0].transpose(1, 2, 0), eye_h).reshape(
        N_HEADS * QK_NOPE, N_HEADS * KV_LORA).astype(BF16)
    w_uv_bd = jnp.einsum('hrv,hg->hrgv', w_uv[0].transpose(1, 0, 2), eye_h).reshape(
        N_HEADS * KV_LORA, D_ATT).astype(BF16)
    a_re, a_im, bb, cc = _s5_prep(ssm_lam_re[0], ssm_lam_im[0], ssm_log_step[0],
                                  ssm_b_re[0], ssm_b_im[0], ssm_c_re[0], ssm_c_im[0])
    mix_w = (row(mix_pre_g), w_in_p, bb, a_re, a_im, cc, row(ssm_d), ssm_w_glu[0].astype(BF16),
             row(ssm_b_glu), row(ssm_out_g), row(q_norm_g), w_uq_p, w_uk_bd, row(kv_norm_g))
    att_g = row(att_out_g)

    cos_p, sin_p = _rope_tables(jnp.arange(seq, dtype=F32))
    x1_p = _ffn1(x_prompt.reshape(bsz * seq, D_MODEL), ffn1_w)
    ys_p, qlat_p, qpe_p, kcat_p, ckv_p, kpe_p, st_p = _mixer(
        x1_p.reshape(bsz, seq, D_MODEL), cos_p, sin_p, jnp.zeros((bsz, 2 * GP), F32), mix_w,
        bsz, min(MIX_TT, seq))
    ya_p = _prompt_attn(qlat_p, qpe_p, kcat_p, w_uv_bd, att_g)
    y_p = _ffn2(x1_p, ys_p.reshape(bsz * seq, D_SSM), ya_p.reshape(bsz * seq, D_ATT), ffn2_w)

    n_s = dbs * dseq
    cos_s, sin_s = _rope_tables(past_len + jnp.arange(dseq, dtype=F32))
    x1_s = _ffn1(x_sample.reshape(n_s, D_MODEL), ffn1_w)
    h0_s = jnp.concatenate([state_ssm_re[0].reshape(dbs, GP), state_ssm_im[0].reshape(dbs, GP)], axis=1)
    ys_s, qlat_s, qpe_s, _, ckv_s, kpe_s, st_s = _mixer(
        x1_s.reshape(1, n_s, D_MODEL), jnp.tile(cos_s, (dbs, 1)), jnp.tile(sin_s, (dbs, 1)), h0_s, mix_w,
        dbs, dseq)
    qlat_d = qlat_s.reshape(dbs, dseq, N_HEADS, KV_LORA).transpose(0, 2, 1, 3).reshape(
        dbs, N_HEADS * dseq, KV_LORA)
    qpe_d = qpe_s.reshape(dbs, dseq, 2, N_HEADS, HALF_ROPE).transpose(0, 3, 1, 2, 4).reshape(
        dbs, N_HEADS * dseq, QK_ROPE)
    cnew = ckv_s.reshape(dbs, dseq, KV_LORA)
    penew = kpe_s.reshape(dbs, dseq, QK_ROPE)
    o_d = _decode_attn(page_table, qlat_d, qpe_d, cnew, penew, cache_ckv, cache_kpe.transpose(0, 1, 3, 2))
    o_cat = o_d.reshape(dbs, N_HEADS, dseq, KV_LORA).transpose(0, 2, 1, 3).reshape(n_s, N_HEADS * KV_LORA)
    ya_s = _uv_proj(o_cat, w_uv_bd, att_g)
    y_s = _ffn2(x1_s, ys_s.reshape(n_s, D_SSM), ya_s, ffn2_w)

    state = lambda st, n: (st[:, :GP].reshape(1, n, N_GROUPS, SSM_STATE),
                           st[:, GP:].reshape(1, n, N_GROUPS, SSM_STATE))
    re_p, im_p = state(st_p, bsz)
    re_s, im_s = state(st_s, dbs)
    return (y_p.reshape(bsz, seq, D_MODEL), y_s.reshape(dbs, dseq, D_MODEL),
            ckv_p[None], kpe_p[None], re_p, im_p,
            cnew[None], penew[None], re_s, im_s)
```

```python
import functools
import math

import jax
import jax.numpy as jnp
from jax import lax
from jax.experimental import pallas as pl
from jax.experimental.pallas import tpu as pltpu

F32 = jnp.float32
BF16 = jnp.bfloat16

D_MODEL = 1024
D_FF = 2816
EPS = 1e-6
D_SSM = 512
SSM_GROUP = 16
N_GROUPS = D_SSM // SSM_GROUP
SSM_STATE = 64
GP = N_GROUPS * SSM_STATE
N_HEADS = 8
QK_NOPE = 64
QK_ROPE = 32
HALF_ROPE = QK_ROPE // 2
V_DIM = 64
D_ATT = N_HEADS * V_DIM
Q_LORA = 256
KV_LORA = 256
ROPE_THETA = 10000.0
SOFTMAX_SCALE = 1.0 / math.sqrt(QK_NOPE + QK_ROPE)
D_IN_PACKED = D_SSM + Q_LORA + KV_LORA + 2 * N_HEADS * HALF_ROPE
D_Q_PACKED = N_HEADS * QK_NOPE + 2 * N_HEADS * HALF_ROPE
PE_W = 2 * N_HEADS * HALF_ROPE

V7X_SUBLANES = 8
V7X_LANES = 128
MXU_N = 256
VMEM_LIMIT = 56 * 1024 * 1024

FFN_TM = 512
FFN_SUB_TILES = 2
MIX_TT = 64
ATT_TQ = 256
ATT_ROW_CHUNKS = 4
DEC_REGION_PAGES = 64
DEC_SEQS_PER_STEP = 2
DEC_AHEAD = 2
DEC_SUB_PAGES = 16
SCAN_SLABS = 4
SCAN_UNROLL = 64
N_SLABS = GP // V7X_LANES


def _const_spec(shape):
    nd = len(shape)
    return pl.BlockSpec(shape, lambda *_: (0,) * nd, pipeline_mode=pl.Buffered(1))


def _rms(x, g):
    return x * lax.rsqrt(jnp.mean(x * x, axis=-1, keepdims=True) + EPS) * g


def _dot(a, b):
    return jnp.dot(a, b, preferred_element_type=F32)


def _dot_nt(a, b):
    return lax.dot_general(a, b, (((1,), (1,)), ((), ())), preferred_element_type=F32)


def _ffn(xs, pre_g, wg_ref, wu_ref, wd_ref, post_g):
    hs = [_rms(x, pre_g).astype(BF16) for x in xs]
    gates = [_dot(h, wg_ref[...]) for h in hs]
    ups = [_dot(h, wu_ref[...]) for h in hs]
    acts = [(gate * jax.nn.sigmoid(gate) * up).astype(BF16) for gate, up in zip(gates, ups)]
    fs = [_dot(act, wd_ref[...]) for act in acts]
    return [x + 0.5 * _rms(f, post_g) for x, f in zip(xs, fs)]


def _params(n_grid_axes):
    return pltpu.CompilerParams(dimension_semantics=("arbitrary",) * n_grid_axes,
                                vmem_limit_bytes=VMEM_LIMIT)


def _sub_tiles(n_rows):
    sub = n_rows // FFN_SUB_TILES if n_rows % (FFN_SUB_TILES * 16) == 0 else n_rows
    return [pl.ds(r0, sub) for r0 in range(0, n_rows, sub)]


def _ffn1_kernel(x_ref, pre_g, wg, wu, wd, post_g, o_ref):
    tiles = _sub_tiles(x_ref.shape[0])
    outs = _ffn([x_ref[rows, :] for rows in tiles], pre_g[...], wg, wu, wd, post_g[...])
    for rows, out in zip(tiles, outs):
        o_ref[rows, :] = out


def _ffn_weight_specs():
    return [_const_spec((1, D_MODEL)), _const_spec((D_MODEL, D_FF)), _const_spec((D_MODEL, D_FF)),
            _const_spec((D_FF, D_MODEL)), _const_spec((1, D_MODEL))]


def _ffn1(x, w):
    r = x.shape[0]
    tm = min(FFN_TM, r)
    rows = pl.BlockSpec((tm, D_MODEL), lambda i: (i, 0))
    return pl.pallas_call(
        _ffn1_kernel,
        grid=(r // tm,),
        in_specs=[rows] + _ffn_weight_specs(),
        out_specs=rows,
        out_shape=jax.ShapeDtypeStruct((r, D_MODEL), F32),
        compiler_params=_params(1),
        name="ffn1",
    )(x, *w)


def _ffn2_kernel(x_ref, ys_ref, ya_ref, wo_s, wo_a, mix_g, pre_g, wg, wu, wd, post_g, o_ref):
    tiles = _sub_tiles(x_ref.shape[0])
    xs = []
    for rows in tiles:
        mixed = _dot(ys_ref[rows, :], wo_s[...]) + _dot(ya_ref[rows, :], wo_a[...])
        xs.append(x_ref[rows, :] + _rms(mixed, mix_g[...]))
    outs = _ffn(xs, pre_g[...], wg, wu, wd, post_g[...])
    for rows, out in zip(tiles, outs):
        o_ref[rows, :] = out


def _ffn2(x1, ys, ya, w):
    r = x1.shape[0]
    tm = min(FFN_TM, r)
    rows = lambda width: pl.BlockSpec((tm, width), lambda i: (i, 0))
    weight_specs = [_const_spec((D_SSM, D_MODEL)), _const_spec((D_ATT, D_MODEL)),
                    _const_spec((1, D_MODEL))] + _ffn_weight_specs()
    return pl.pallas_call(
        _ffn2_kernel,
        grid=(r // tm,),
        in_specs=[rows(D_MODEL), rows(D_SSM), rows(D_ATT)] + weight_specs,
        out_specs=rows(D_MODEL),
        out_shape=jax.ShapeDtypeStruct((r, D_MODEL), F32),
        compiler_params=_params(1),
        name="ffn2",
    )(x1, ys, ya, *w)


def _s5_prep_kernel(lr_ref, li_ref, ls_ref, br_ref, bi_ref, cr_ref, ci_ref,
                    are_ref, aim_ref, bb_ref, cc_ref):
    lr = lr_ref[...]
    li = li_ref[...]
    dt = jnp.exp(ls_ref[...])
    mag = jnp.exp(lr * dt)
    a_re = mag * jnp.cos(li * dt)
    a_im = mag * jnp.sin(li * dt)
    den = lr * lr + li * li
    n_re = a_re - 1.0
    z_re = (n_re * lr + a_im * li) / den
    z_im = (a_im * lr - n_re * li) / den
    are_ref[...] = a_re
    aim_ref[...] = a_im

    row_g = lax.shift_right_logical(lax.broadcasted_iota(jnp.int32, (D_SSM, GP), 0), 4)
    col_g = lax.shift_right_logical(lax.broadcasted_iota(jnp.int32, (D_SSM, GP), 1), 6)
    same = row_g == col_g
    br = br_ref[...]
    bi = bi_ref[...]
    bb_ref[:, :GP] = jnp.where(same, z_re * br - z_im * bi, 0.0).astype(BF16)
    bb_ref[:, GP:] = jnp.where(same, z_re * bi + z_im * br, 0.0).astype(BF16)

    row_g = lax.shift_right_logical(lax.broadcasted_iota(jnp.int32, (GP, D_SSM), 0), 6)
    col_g = lax.shift_right_logical(lax.broadcasted_iota(jnp.int32, (GP, D_SSM), 1), 4)
    same = row_g == col_g
    cc_ref[:GP, :] = jnp.where(same, cr_ref[...], 0.0).astype(BF16)
    cc_ref[GP:, :] = jnp.where(same, -ci_ref[...], 0.0).astype(BF16)


def _s5_prep(lam_re, lam_im, log_step, b_re, b_im, c_re, c_im):
    lr = lam_re.reshape(1, GP)
    li = lam_im.reshape(1, GP)
    ls = jnp.repeat(log_step, SSM_STATE).reshape(1, GP)
    br = jnp.tile(b_re.transpose(2, 0, 1).reshape(SSM_GROUP, GP), (N_GROUPS, 1))
    bi = jnp.tile(b_im.transpose(2, 0, 1).reshape(SSM_GROUP, GP), (N_GROUPS, 1))
    cr = jnp.tile(c_re.transpose(0, 2, 1).reshape(GP, SSM_GROUP), (1, N_GROUPS))
    ci = jnp.tile(c_im.transpose(0, 2, 1).reshape(GP, SSM_GROUP), (1, N_GROUPS))
    return pl.pallas_call(
        _s5_prep_kernel,
        out_shape=(jax.ShapeDtypeStruct((1, GP), F32), jax.ShapeDtypeStruct((1, GP), F32),
                   jax.ShapeDtypeStruct((D_SSM, 2 * GP), BF16), jax.ShapeDtypeStruct((2 * GP, D_SSM), BF16)),
        compiler_params=pltpu.CompilerParams(vmem_limit_bytes=VMEM_LIMIT),
        name="s5_prep",
    )(lr, li, ls, br, bi, cr, ci)


def _scan_time_major(nb, tt):
    return tt % V7X_SUBLANES == 0 and nb % V7X_SUBLANES == 0


def _scan_pitch(tt):
    return tt if tt % 16 else tt + V7X_SUBLANES


def _mixer_kernel(nb, tt,
                  x_ref, cos_ref, sin_ref, h0_ref, pre_g, w_in, bb, are_ref, aim_ref, cc, d_skip,
                  w_glu, b_glu, ssm_g, q_g, w_uq, w_uk, kv_g,
                  ys_ref, qlat_ref, qpe_ref, kcat_ref, ckv_ref, kpe_ref, st_ref,
                  hbuf, tbuf):
    g, rpg, _ = x_ref.shape
    r = nb * tt
    time_major = _scan_time_major(nb, tt)
    pitch = _scan_pitch(tt)

    @pl.when(pl.program_id(0) == 0)
    def _():
        st_ref[...] = h0_ref[...]

    x = x_ref[...].reshape(r, D_MODEL)
    h = _rms(x, pre_g[...]).astype(BF16)
    z = _dot(h, w_in[...])
    u = z[:, :D_SSM]
    hq = z[:, D_SSM:D_SSM + Q_LORA]
    hkv = z[:, D_SSM + Q_LORA:D_SSM + Q_LORA + KV_LORA]
    hpe = z[:, D_SSM + Q_LORA + KV_LORA:]

    cos = cos_ref[...]
    sin = sin_ref[...]

    def rope(x1, x2):
        x1 = x1.reshape(g, rpg, V7X_LANES)
        x2 = x2.reshape(g, rpg, V7X_LANES)
        return x1 * cos - x2 * sin, x2 * cos + x1 * sin

    q = _dot(_rms(hq, q_g[...]).astype(BF16), w_uq[...]) * SOFTMAX_SCALE
    q_nope = q[:, :N_HEADS * QK_NOPE]
    q1, q2 = rope(q[:, N_HEADS * QK_NOPE:N_HEADS * QK_NOPE + V7X_LANES],
                  q[:, N_HEADS * QK_NOPE + V7X_LANES:])
    qpe_ref[:, :, :V7X_LANES] = q1.astype(BF16)
    qpe_ref[:, :, V7X_LANES:] = q2.astype(BF16)
    q_nope = q_nope.astype(BF16)
    for hd in range(N_HEADS):
        k0 = (hd * QK_NOPE // V7X_LANES) * V7X_LANES
        cols = slice(hd * KV_LORA, (hd + 1) * KV_LORA)
        qlat_ref[:, :, cols] = _dot(q_nope[:, k0:k0 + V7X_LANES], w_uk[k0:k0 + V7X_LANES, cols]).astype(
            BF16).reshape(g, rpg, KV_LORA)

    ckv = _rms(hkv, kv_g[...]).reshape(g, rpg, KV_LORA)
    ckv_ref[...] = ckv
    k1, k2 = rope(hpe[:, :V7X_LANES], hpe[:, V7X_LANES:])
    kcat_ref[:, :, :KV_LORA] = ckv.astype(BF16)
    kcat_ref[:, :, KV_LORA:KV_LORA + V7X_LANES] = k1.astype(BF16)
    kcat_ref[:, :, KV_LORA + V7X_LANES:] = k2.astype(BF16)
    kpe_ref[:, :, :HALF_ROPE] = k1[:, :, :HALF_ROPE]
    kpe_ref[:, :, HALF_ROPE:] = k2[:, :, :HALF_ROPE]

    n_lane_tiles = D_SSM // V7X_LANES
    if time_major:
        for c in range(n_lane_tiles):
            for b in range(nb):
                tbuf[c, b * pitch:b * pitch + tt, :] = u[b * tt:(b + 1) * tt, c * V7X_LANES:(c + 1) * V7X_LANES]
        u_rows = jnp.concatenate(
            [jnp.concatenate([tbuf[c, pl.ds(t, nb, stride=pitch), :] for t in range(tt)], axis=0)
             for c in range(n_lane_tiles)], axis=1)

        def step_rows(t, sg):
            return pl.ds(pl.multiple_of(t * nb + sg * V7X_SUBLANES, V7X_SUBLANES), V7X_SUBLANES)
    else:
        u_rows = u

        def step_rows(t, sg):
            return pl.ds(sg * (V7X_SUBLANES * tt) + t, V7X_SUBLANES, stride=tt)

    u_bf = u_rows.astype(BF16)
    tiles_per_part = GP // MXU_N
    for j in range(2 * tiles_per_part):
        k0 = ((j % tiles_per_part) // 2) * V7X_LANES
        bu = _dot(u_bf[:, k0:k0 + V7X_LANES], bb[k0:k0 + V7X_LANES, j * MXU_N:(j + 1) * MXU_N])
        for half in range(MXU_N // V7X_LANES):
            hbuf[j * (MXU_N // V7X_LANES) + half] = bu[:, half * V7X_LANES:(half + 1) * V7X_LANES]

    n_sub = nb // V7X_SUBLANES
    for lc in range(N_SLABS // SCAN_SLABS):
        slabs = [lc * SCAN_SLABS + k for k in range(SCAN_SLABS)]
        a_re = [are_ref[:, s * V7X_LANES:(s + 1) * V7X_LANES] for s in slabs]
        a_im = [aim_ref[:, s * V7X_LANES:(s + 1) * V7X_LANES] for s in slabs]

        def sub_body(sg, carry, slabs=slabs, a_re=a_re, a_im=a_im):
            seqs = pl.ds(pl.multiple_of(sg * V7X_SUBLANES, V7X_SUBLANES), V7X_SUBLANES)

            def t_body(t, state):
                rows = step_rows(t, sg)
                new = []
                for k, s in enumerate(slabs):
                    s_re, s_im = state[2 * k], state[2 * k + 1]
                    n_re = a_re[k] * s_re - a_im[k] * s_im + hbuf[s, rows, :]
                    n_im = a_re[k] * s_im + a_im[k] * s_re + hbuf[N_SLABS + s, rows, :]
                    hbuf[s, rows, :] = n_re
                    hbuf[N_SLABS + s, rows, :] = n_im
                    new += [n_re, n_im]
                return tuple(new)

            init = []
            for s in slabs:
                init += [st_ref[seqs, s * V7X_LANES:(s + 1) * V7X_LANES],
                         st_ref[seqs, GP + s * V7X_LANES:GP + (s + 1) * V7X_LANES]]
            fin = lax.fori_loop(0, tt, t_body, tuple(init), unroll=min(tt, SCAN_UNROLL))
            for k, s in enumerate(slabs):
                st_ref[seqs, s * V7X_LANES:(s + 1) * V7X_LANES] = fin[2 * k]
                st_ref[seqs, GP + s * V7X_LANES:GP + (s + 1) * V7X_LANES] = fin[2 * k + 1]
            return carry

        lax.fori_loop(0, n_sub, sub_body, 0)

    y_tiles = []
    k_per_tile = GP * MXU_N // D_SSM
    slabs_per_tile = k_per_tile // V7X_LANES
    for n in range(D_SSM // MXU_N):
        cols = slice(n * MXU_N, (n + 1) * MXU_N)
        acc = None
        for part in range(2):
            s0 = part * N_SLABS + n * slabs_per_tile
            hs = jnp.concatenate([hbuf[s0 + c] for c in range(slabs_per_tile)], axis=1).astype(BF16)
            k0 = part * GP + n * k_per_tile
            term = _dot(hs, cc[k0:k0 + k_per_tile, cols])
            acc = term if acc is None else acc + term
        y_tiles.append(acc)
    y = jnp.concatenate(y_tiles, axis=1)
    if time_major:
        for c in range(n_lane_tiles):
            tbuf[c, 0:r, :] = y[:, c * V7X_LANES:(c + 1) * V7X_LANES]
        y = jnp.concatenate(
            [jnp.concatenate([tbuf[c, pl.ds(b, tt, stride=nb), :] for b in range(nb)], axis=0)
             for c in range(n_lane_tiles)], axis=1)
    y = y + d_skip[...] * u
    y = jax.nn.gelu(y, approximate=True)
    y = y * jax.nn.sigmoid(_dot(y.astype(BF16), w_glu[...]) + b_glu[...])
    ys_ref[...] = _rms(y, ssm_g[...]).astype(BF16).reshape(g, rpg, D_SSM)


def _mixer(x1, cos, sin, h0, w, nb, tt):
    n_g, length, _ = x1.shape
    rpg = nb * tt // n_g
    assert length % rpg == 0 and rpg % 16 == 0 and nb % V7X_SUBLANES == 0
    tbuf_rows = nb * _scan_pitch(tt) if _scan_time_major(nb, tt) else V7X_SUBLANES
    row = lambda width: pl.BlockSpec((n_g, rpg, width), lambda i: (0, i, 0))
    out = lambda width, dtype: jax.ShapeDtypeStruct((n_g, length, width), dtype)
    weight_specs = [
        _const_spec((1, D_MODEL)), _const_spec((D_MODEL, D_IN_PACKED)), _const_spec((D_SSM, 2 * GP)),
        _const_spec((1, GP)), _const_spec((1, GP)), _const_spec((2 * GP, D_SSM)), _const_spec((1, D_SSM)),
        _const_spec((D_SSM, D_SSM)), _const_spec((1, D_SSM)), _const_spec((1, D_SSM)),
        _const_spec((1, Q_LORA)), _const_spec((Q_LORA, D_Q_PACKED)),
        _const_spec((N_HEADS * QK_NOPE, N_HEADS * KV_LORA)), _const_spec((1, KV_LORA)),
    ]
    rope_spec = pl.BlockSpec((rpg, V7X_LANES), lambda i: (i, 0))
    return pl.pallas_call(
        functools.partial(_mixer_kernel, nb, tt),
        grid=(length // rpg,),
        in_specs=[row(D_MODEL), rope_spec, rope_spec, _const_spec((nb, 2 * GP))] + weight_specs,
        out_specs=[row(D_SSM), row(N_HEADS * KV_LORA), row(PE_W), row(KV_LORA + PE_W), row(KV_LORA),
                   row(QK_ROPE), pl.BlockSpec((nb, 2 * GP), lambda i: (0, 0))],
        out_shape=[out(D_SSM, BF16), out(N_HEADS * KV_LORA, BF16), out(PE_W, BF16),
                   out(KV_LORA + PE_W, BF16), out(KV_LORA, F32), out(QK_ROPE, F32),
                   jax.ShapeDtypeStruct((nb, 2 * GP), F32)],
        scratch_shapes=[pltpu.VMEM((2 * N_SLABS, nb * tt, V7X_LANES), F32),
                        pltpu.VMEM((D_SSM // V7X_LANES, tbuf_rows, V7X_LANES), F32)],
        compiler_params=_params(1),
        name="mixer",
    )(x1, cos, sin, h0, *w)


def _uv_project(o_heads, w_uv):
    heads_per_tile = MXU_N // V_DIM
    tiles = []
    for n in range(D_ATT // MXU_N):
        h0 = n * heads_per_tile
        lhs = jnp.concatenate(o_heads[h0:h0 + heads_per_tile], axis=-1)
        tiles.append(_dot(lhs, w_uv[h0 * KV_LORA:(h0 + heads_per_tile) * KV_LORA,
                                    n * MXU_N:(n + 1) * MXU_N]))
    return jnp.concatenate(tiles, axis=-1)


def _prompt_attn_kernel(tq, qlat_ref, qpe_ref, kcat_ref, w_uv, att_g, ya_ref, acc_ref):
    i = pl.program_id(1)
    qpe = qpe_ref[...]
    head_of_lane = lax.shift_right_logical(
        lax.broadcasted_iota(jnp.int32, (tq, PE_W), 1) & (V7X_LANES - 1), 4)
    q_heads = []
    for hd in range(N_HEADS):
        q_heads.append(jnp.concatenate(
            [qlat_ref[:, hd * KV_LORA:(hd + 1) * KV_LORA],
             jnp.where(head_of_lane == hd, qpe, jnp.zeros_like(qpe))], axis=-1))
    hpc = N_HEADS // ATT_ROW_CHUNKS
    rows = hpc * tq
    qs = [jnp.concatenate(q_heads[c * hpc:(c + 1) * hpc], axis=0) for c in range(ATT_ROW_CHUNKS)]

    def blocks(key_blocks, state):
        ks = [kcat_ref[pl.ds(pl.multiple_of(j * tq, tq), tq), :] for j, _ in key_blocks]
        scores = [[_dot_nt(q, k) for k in ks] for q in qs]
        new_state = []
        for c in range(ATT_ROW_CHUNKS):
            m, l = state[2 * c], state[2 * c + 1]
            ss = []
            for s, (_, diagonal) in zip(scores[c], key_blocks):
                if diagonal:
                    q_pos = lax.broadcasted_iota(jnp.int32, (rows, tq), 0) & (tq - 1)
                    k_pos = lax.broadcasted_iota(jnp.int32, (rows, tq), 1)
                    s = jnp.where(k_pos <= q_pos, s, -jnp.inf)
                ss.append(s)
            m_new = m
            for s in ss:
                m_new = jnp.maximum(m_new, jnp.max(s, axis=-1, keepdims=True))
            alpha = jnp.exp(m - m_new)
            l_new = alpha * l
            pv = None
            for s, k in zip(ss, ks):
                p = jnp.exp(s - m_new)
                l_new = l_new + jnp.sum(p, axis=-1, keepdims=True)
                term = _dot(p.astype(BF16), k[:, :KV_LORA])
                pv = term if pv is None else pv + term
            acc_rows = pl.ds(c * rows, rows)
            acc_ref[acc_rows, :] = alpha * acc_ref[acc_rows, :] + pv
            new_state += [m_new, l_new]
        return tuple(new_state)

    acc_ref[...] = jnp.zeros_like(acc_ref)
    init = (jnp.full((rows, 1), -jnp.inf, F32), jnp.zeros((rows, 1), F32)) * ATT_ROW_CHUNKS
    state = lax.fori_loop(0, i // 2, lambda j, st: blocks([(2 * j, False), (2 * j + 1, False)], st), init)
    state = lax.cond(i % 2 == 1,
                     lambda st: blocks([(i - 1, False), (i, True)], st),
                     lambda st: blocks([(i, True)], st), state)

    l_all = jnp.concatenate([state[2 * c + 1] for c in range(ATT_ROW_CHUNKS)], axis=0)
    o = acc_ref[...] / l_all
    o = o.astype(BF16)
    o_heads = [o[hd * tq:(hd + 1) * tq, :] for hd in range(N_HEADS)]
    ya_ref[...] = _rms(_uv_project(o_heads, w_uv), att_g[...]).astype(BF16)


def _prompt_attn(qlat, qpe, kcat, w_uv, att_g):
    bsz, t, _ = qlat.shape
    tq = min(ATT_TQ, t)
    assert t % tq == 0 and tq & (tq - 1) == 0
    return pl.pallas_call(
        functools.partial(_prompt_attn_kernel, tq),
        grid=(bsz, t // tq),
        in_specs=[pl.BlockSpec((None, tq, N_HEADS * KV_LORA), lambda b, i: (b, i, 0)),
                  pl.BlockSpec((None, tq, PE_W), lambda b, i: (b, i, 0)),
                  pl.BlockSpec((None, t, KV_LORA + PE_W), lambda b, i: (b, 0, 0)),
                  _const_spec((N_HEADS * KV_LORA, D_ATT)), _const_spec((1, D_ATT))],
        out_specs=pl.BlockSpec((None, tq, D_ATT), lambda b, i: (b, i, 0)),
        out_shape=jax.ShapeDtypeStruct((bsz, t, D_ATT), BF16),
        scratch_shapes=[pltpu.VMEM((N_HEADS * tq, KV_LORA), F32)],
        compiler_params=_params(2),
        name="prompt_attn",
    )(qlat, qpe, kcat, w_uv, att_g)


def _merge(m, l, acc, m_c, l_c, acc_c):
    m_new = jnp.maximum(m, m_c)
    w = jnp.exp(m - m_new)
    w_c = jnp.exp(m_c - m_new)
    return m_new, w * l + w_c * l_c, w * acc + w_c * acc_c


def _new_token_state(qlat, qpe, cnew_ref, penew_ref, seq, dec_t):
    rows = qlat.shape[0]
    qlat_f = qlat.astype(F32)
    qpe_f = qpe.astype(F32)
    t_of_row = lax.broadcasted_iota(jnp.int32, (rows, 1), 0) & (dec_t - 1)
    s_new = []
    for j in range(dec_t):
        sj = (jnp.sum(qlat_f * cnew_ref[seq, j:j + 1, :], axis=-1, keepdims=True)
              + jnp.sum(qpe_f * penew_ref[seq, j:j + 1, :], axis=-1, keepdims=True))
        s_new.append(jnp.where(t_of_row >= j, sj, -jnp.inf))
    m = s_new[0]
    for j in range(1, dec_t):
        m = jnp.maximum(m, s_new[j])
    l = jnp.zeros((rows, 1), F32)
    acc = jnp.zeros((rows, KV_LORA), F32)
    for j in range(dec_t):
        pj = jnp.exp(s_new[j] - m)
        l = l + pj
        acc = acc + pj * cnew_ref[seq, j:j + 1, :]
    return m, l, acc


def _decode_attn_kernel(n_regions, dec_t,
                        pt_ref, qlat_ref, qpe_ref, cnew_ref, penew_ref, ckv_hbm, kpe_hbm,
                        o_ref, kvbuf, pebuf, sem):
    step = pl.program_id(0)
    n_steps = pl.num_programs(0)
    n_slots = DEC_SEQS_PER_STEP * n_regions

    def page_copies(st, slot, p):
        seq = st * DEC_SEQS_PER_STEP + slot // n_regions
        page = pt_ref[seq, (slot % n_regions) * DEC_REGION_PAGES + p]
        return (pltpu.make_async_copy(ckv_hbm.at[0, page], kvbuf.at[slot, p], sem.at[0, slot]),
                pltpu.make_async_copy(kpe_hbm.at[0, page], pebuf.at[slot, p], sem.at[1, slot]))

    def start_region(st, slot):
        for p in range(DEC_REGION_PAGES):
            for cp in page_copies(st, slot, p):
                cp.start()

    def wait_region(st, slot):
        for p in range(DEC_REGION_PAGES):
            for cp in page_copies(st, slot, p):
                cp.wait()

    @pl.when(step == 0)
    def _():
        for slot in range(DEC_AHEAD):
            start_region(0, slot)

    sub_keys = DEC_SUB_PAGES * V7X_LANES
    for slot in range(n_slots):
        seq = slot // n_regions
        if slot % n_regions == 0:
            qlat = qlat_ref[seq]
            qpe = qpe_ref[seq]
            m, l, acc = _new_token_state(qlat, qpe, cnew_ref, penew_ref, seq, dec_t)
        ahead = slot + DEC_AHEAD
        if ahead < n_slots:
            start_region(step, ahead)
        else:
            @pl.when(step + 1 < n_steps)
            def _(ahead=ahead):
                start_region(step + 1, ahead - n_slots)
        wait_region(step, slot)
        kvs, scores = [], []
        for sub in range(DEC_REGION_PAGES // DEC_SUB_PAGES):
            p0 = sub * DEC_SUB_PAGES
            kv = kvbuf[slot, p0:p0 + DEC_SUB_PAGES].reshape(sub_keys, KV_LORA).astype(BF16)
            pe_t = jnp.concatenate([pebuf[slot, p0 + p] for p in range(DEC_SUB_PAGES)],
                                   axis=1).astype(BF16)
            kvs.append(kv)
            scores.append(_dot_nt(qlat, kv) + _dot(qpe, pe_t))
        for kv, s in zip(kvs, scores):
            m_c = jnp.max(s, axis=-1, keepdims=True)
            p = jnp.exp(s - m_c)
            l_c = jnp.sum(p, axis=-1, keepdims=True)
            acc_c = _dot(p.astype(BF16), kv)
            m, l, acc = _merge(m, l, acc, m_c, l_c, acc_c)
        if slot % n_regions == n_regions - 1:
            o_ref[seq] = acc / l


def _decode_attn(page_table, qlat, qpe, cnew, penew, cache_ckv, cache_kpe_t):
    bsz, n_pages = page_table.shape
    dec_t = cnew.shape[1]
    page = cache_ckv.shape[2]
    assert page == V7X_LANES and n_pages % DEC_REGION_PAGES == 0 and dec_t & (dec_t - 1) == 0
    assert bsz % DEC_SEQS_PER_STEP == 0
    rows = N_HEADS * dec_t
    n_regions = n_pages // DEC_REGION_PAGES
    n_slots = DEC_SEQS_PER_STEP * n_regions
    assert n_slots > DEC_AHEAD, "a slot must not be refilled while it is still being read"
    per_step = lambda r, width: pl.BlockSpec((DEC_SEQS_PER_STEP, r, width), lambda s, pt: (s, 0, 0))
    grid_spec = pltpu.PrefetchScalarGridSpec(
        num_scalar_prefetch=1,
        grid=(bsz // DEC_SEQS_PER_STEP,),
        in_specs=[per_step(rows, KV_LORA), per_step(rows, QK_ROPE), per_step(dec_t, KV_LORA),
                  per_step(dec_t, QK_ROPE),
                  pl.BlockSpec(memory_space=pl.ANY), pl.BlockSpec(memory_space=pl.ANY)],
        out_specs=per_step(rows, KV_LORA),
        scratch_shapes=[pltpu.VMEM((n_slots, DEC_REGION_PAGES, page, KV_LORA), F32),
                        pltpu.VMEM((n_slots, DEC_REGION_PAGES, QK_ROPE, page), F32),
                        pltpu.SemaphoreType.DMA((2, n_slots))],
    )
    return pl.pallas_call(
        functools.partial(_decode_attn_kernel, n_regions, dec_t),
        grid_spec=grid_spec,
        out_shape=jax.ShapeDtypeStruct((bsz, rows, KV_LORA), F32),
        compiler_params=_params(1),
        name="decode_attn",
    )(page_table, qlat, qpe, cnew, penew, cache_ckv, cache_kpe_t)


def _uv_kernel(o_ref, w_uv, att_g, ya_ref):
    o_heads = [o_ref[:, hd * KV_LORA:(hd + 1) * KV_LORA].astype(BF16) for hd in range(N_HEADS)]
    ya_ref[...] = _rms(_uv_project(o_heads, w_uv), att_g[...]).astype(BF16)


def _uv_proj(o_cat, w_uv, att_g):
    r = o_cat.shape[0]
    return pl.pallas_call(
        _uv_kernel,
        out_shape=jax.ShapeDtypeStruct((r, D_ATT), BF16),
        compiler_params=pltpu.CompilerParams(vmem_limit_bytes=VMEM_LIMIT),
        name="uv_proj",
    )(o_cat, w_uv, att_g)


def _rope_tables(pos):
    inv = ROPE_THETA ** (-jnp.arange(0, QK_ROPE, 2, dtype=F32) / QK_ROPE)
    ang = pos[:, None] * inv[None, :]
    return jnp.tile(jnp.cos(ang), (1, N_HEADS)), jnp.tile(jnp.sin(ang), (1, N_HEADS))


def kernel(x_prompt, x_sample, cache_ckv, cache_kpe, state_ssm_re, state_ssm_im, page_table, ffn1_pre_g, ffn1_w_gate, ffn1_w_up, ffn1_w_down, ffn1_post_g, mix_pre_g, w_in, ssm_lam_re, ssm_lam_im, ssm_log_step, ssm_b_re, ssm_b_im, ssm_c_re, ssm_c_im, ssm_d, ssm_w_glu, ssm_b_glu, q_norm_g, w_uq, kv_norm_g, w_uk, w_uv, ssm_out_g, att_out_g, w_out, mix_post_g, ffn2_pre_g, ffn2_w_gate, ffn2_w_up, ffn2_w_down, ffn2_post_g):
    assert ffn1_pre_g.shape[0] == 1, "single-layer trunk"
    bsz, seq, _ = x_prompt.shape
    dbs, dseq, _ = x_sample.shape
    past_len = page_table.shape[1] * cache_ckv.shape[2]
    assert bsz == V7X_SUBLANES

    row = lambda v: v[0].reshape(1, -1)
    ffn1_w = (row(ffn1_pre_g), ffn1_w_gate[0].astype(BF16), ffn1_w_up[0].astype(BF16),
              ffn1_w_down[0].astype(BF16), row(ffn1_post_g))
    ffn2_w = (w_out[0, :D_SSM].astype(BF16), w_out[0, D_SSM:].astype(BF16), row(mix_post_g),
              row(ffn2_pre_g), ffn2_w_gate[0].astype(BF16), ffn2_w_up[0].astype(BF16),
              ffn2_w_down[0].astype(BF16), row(ffn2_post_g))
    off_pe = D_SSM + Q_LORA + KV_LORA
    w_in_p = jnp.concatenate(
        [w_in[0, :, :off_pe],
         jnp.tile(w_in[0, :, off_pe:off_pe + HALF_ROPE], (1, N_HEADS)),
         jnp.tile(w_in[0, :, off_pe + HALF_ROPE:], (1, N_HEADS))], axis=1).astype(BF16)
    wq = w_uq[0]
    w_uq_p = jnp.concatenate(
        [wq[:, :, :QK_NOPE].reshape(Q_LORA, N_HEADS * QK_NOPE),
         wq[:, :, QK_NOPE:QK_NOPE + HALF_ROPE].reshape(Q_LORA, N_HEADS * HALF_ROPE),
         wq[:, :, QK_NOPE + HALF_ROPE:].reshape(Q_LORA, N_HEADS * HALF_ROPE)], axis=1).astype(BF16)
    eye_h = jnp.eye(N_HEADS, dtype=F32)
    w_uk_bd = jnp.einsum('hnr,hg->hngr', w_uk[0].transpose(1, 2, 0), eye_h).reshape(
        N_HEADS * QK_NOPE, N_HEADS * KV_LORA).astype(BF16)
    w_uv_bd = jnp.einsum('hrv,hg->hrgv', w_uv[0].transpose(1, 0, 2), eye_h).reshape(
        N_HEADS * KV_LORA, D_ATT).astype(BF16)
    a_re, a_im, bb, cc = _s5_prep(ssm_lam_re[0], ssm_lam_im[0], ssm_log_step[0],
                                  ssm_b_re[0], ssm_b_im[0], ssm_c_re[0], ssm_c_im[0])
    mix_w = (row(mix_pre_g), w_in_p, bb, a_re, a_im, cc, row(ssm_d), ssm_w_glu[0].astype(BF16),
             row(ssm_b_glu), row(ssm_out_g), row(q_norm_g), w_uq_p, w_uk_bd, row(kv_norm_g))
    att_g = row(att_out_g)

    cos_p, sin_p = _rope_tables(jnp.arange(seq, dtype=F32))
    x1_p = _ffn1(x_prompt.reshape(bsz * seq, D_MODEL), ffn1_w)
    ys_p, qlat_p, qpe_p, kcat_p, ckv_p, kpe_p, st_p = _mixer(
        x1_p.reshape(bsz, seq, D_MODEL), cos_p, sin_p, jnp.zeros((bsz, 2 * GP), F32), mix_w,
        bsz, min(MIX_TT, seq))
    ya_p = _prompt_attn(qlat_p, qpe_p, kcat_p, w_uv_bd, att_g)
    y_p = _ffn2(x1_p, ys_p.reshape(bsz * seq, D_SSM), ya_p.reshape(bsz * seq, D_ATT), ffn2_w)

    n_s = dbs * dseq
    cos_s, sin_s = _rope_tables(past_len + jnp.arange(dseq, dtype=F32))
    x1_s = _ffn1(x_sample.reshape(n_s, D_MODEL), ffn1_w)
    h0_s = jnp.concatenate([state_ssm_re[0].reshape(dbs, GP), state_ssm_im[0].reshape(dbs, GP)], axis=1)
    ys_s, qlat_s, qpe_s, _, ckv_s, kpe_s, st_s = _mixer(
        x1_s.reshape(1, n_s, D_MODEL), jnp.tile(cos_s, (dbs, 1)), jnp.tile(sin_s, (dbs, 1)), h0_s, mix_w,
        dbs, dseq)
    qlat_d = qlat_s.reshape(dbs, dseq, N_HEADS, KV_LORA).transpose(0, 2, 1, 3).reshape(
        dbs, N_HEADS * dseq, KV_LORA)
    qpe_d = qpe_s.reshape(dbs, dseq, 2, N_HEADS, HALF_ROPE).transpose(0, 3, 1, 2, 4).reshape(
        dbs, N_HEADS * dseq, QK_ROPE)
    cnew = ckv_s.reshape(dbs, dseq, KV_LORA)
    penew = kpe_s.reshape(dbs, dseq, QK_ROPE)
    o_d = _decode_attn(page_table, qlat_d, qpe_d, cnew, penew, cache_ckv, cache_kpe.transpose(0, 1, 3, 2))
    o_cat = o_d.reshape(dbs, N_HEADS, dseq, KV_LORA).transpose(0, 2, 1, 3).reshape(n_s, N_HEADS * KV_LORA)
    ya_s = _uv_proj(o_cat, w_uv_bd, att_g)
    y_s = _ffn2(x1_s, ys_s.reshape(n_s, D_SSM), ya_s, ffn2_w)

    state = lambda st, n: (st[:, :GP].reshape(1, n, N_GROUPS, SSM_STATE),
                           st[:, GP:].reshape(1, n, N_GROUPS, SSM_STATE))
    re_p, im_p = state(st_p, bsz)
    re_s, im_s = state(st_s, dbs)
    return (y_p.reshape(bsz, seq, D_MODEL), y_s.reshape(dbs, dseq, D_MODEL),
            ckv_p[None], kpe_p[None], re_p, im_p,
            cnew[None], penew[None], re_s, im_s)
```

```python
import functools
import math

import jax
import jax.numpy as jnp
from jax import lax
from jax.experimental import pallas as pl
from jax.experimental.pallas import tpu as pltpu

F32 = jnp.float32
BF16 = jnp.bfloat16

D_MODEL = 1024
D_FF = 2816
EPS = 1e-6
D_SSM = 512
SSM_GROUP = 16
N_GROUPS = D_SSM // SSM_GROUP
SSM_STATE = 64
GP = N_GROUPS * SSM_STATE
N_HEADS = 8
QK_NOPE = 64
QK_ROPE = 32
HALF_ROPE = QK_ROPE // 2
V_DIM = 64
D_ATT = N_HEADS * V_DIM
Q_LORA = 256
KV_LORA = 256
ROPE_THETA = 10000.0
SOFTMAX_SCALE = 1.0 / math.sqrt(QK_NOPE + QK_ROPE)
D_IN_PACKED = D_SSM + Q_LORA + KV_LORA + 2 * N_HEADS * HALF_ROPE
D_Q_PACKED = N_HEADS * QK_NOPE + 2 * N_HEADS * HALF_ROPE
PE_W = 2 * N_HEADS * HALF_ROPE

V7X_SUBLANES = 8
V7X_LANES = 128
MXU_N = 256
VMEM_LIMIT = 56 * 1024 * 1024

FFN_TM = 512
FFN_SUB_TILES = 2
MIX_TT = 64
ATT_TQ = 256
ATT_ROW_CHUNKS = 4
DEC_REGION_PAGES = 64
DEC_SEQS_PER_STEP = 2
DEC_AHEAD = 2
DEC_SUB_PAGES = 16
SCAN_SLABS = 4
SCAN_UNROLL = 64
N_SLABS = GP // V7X_LANES


def _const_spec(shape):
    nd = len(shape)
    return pl.BlockSpec(shape, lambda *_: (0,) * nd, pipeline_mode=pl.Buffered(1))


def _rms(x, g):
    return x * lax.rsqrt(jnp.mean(x * x, axis=-1, keepdims=True) + EPS) * g


def _dot(a, b):
    return jnp.dot(a, b, preferred_element_type=F32)


def _dot_nt(a, b):
    return lax.dot_general(a, b, (((1,), (1,)), ((), ())), preferred_element_type=F32)


def _ffn(xs, pre_g, wg_ref, wu_ref, wd_ref, post_g):
    hs = [_rms(x, pre_g).astype(BF16) for x in xs]
    gates = [_dot(h, wg_ref[...]) for h in hs]
    ups = [_dot(h, wu_ref[...]) for h in hs]
    acts = [(gate * jax.nn.sigmoid(gate) * up).astype(BF16) for gate, up in zip(gates, ups)]
    fs = [_dot(act, wd_ref[...]) for act in acts]
    return [x + 0.5 * _rms(f, post_g) for x, f in zip(xs, fs)]


def _params(n_grid_axes):
    return pltpu.CompilerParams(dimension_semantics=("arbitrary",) * n_grid_axes,
                                vmem_limit_bytes=VMEM_LIMIT)


def _sub_tiles(n_rows):
    sub = n_rows // FFN_SUB_TILES if n_rows % (FFN_SUB_TILES * 16) == 0 else n_rows
    return [pl.ds(r0, sub) for r0 in range(0, n_rows, sub)]


def _ffn1_kernel(x_ref, pre_g, wg, wu, wd, post_g, o_ref):
    tiles = _sub_tiles(x_ref.shape[0])
    outs = _ffn([x_ref[rows, :] for rows in tiles], pre_g[...], wg, wu, wd, post_g[...])
    for rows, out in zip(tiles, outs):
        o_ref[rows, :] = out


def _ffn_weight_specs():
    return [_const_spec((1, D_MODEL)), _const_spec((D_MODEL, D_FF)), _const_spec((D_MODEL, D_FF)),
            _const_spec((D_FF, D_MODEL)), _const_spec((1, D_MODEL))]


def _ffn1(x, w):
    r = x.shape[0]
    tm = min(FFN_TM, r)
    rows = pl.BlockSpec((tm, D_MODEL), lambda i: (i, 0))
    return pl.pallas_call(
        _ffn1_kernel,
        grid=(r // tm,),
        in_specs=[rows] + _ffn_weight_specs(),
        out_specs=rows,
        out_shape=jax.ShapeDtypeStruct((r, D_MODEL), F32),
        compiler_params=_params(1),
        name="ffn1",
    )(x, *w)


def _ffn2_kernel(x_ref, ys_ref, ya_ref, wo_s, wo_a, mix_g, pre_g, wg, wu, wd, post_g, o_ref):
    tiles = _sub_tiles(x_ref.shape[0])
    xs = []
    for rows in tiles:
        mixed = _dot(ys_ref[rows, :], wo_s[...]) + _dot(ya_ref[rows, :], wo_a[...])
        xs.append(x_ref[rows, :] + _rms(mixed, mix_g[...]))
    outs = _ffn(xs, pre_g[...], wg, wu, wd, post_g[...])
    for rows, out in zip(tiles, outs):
        o_ref[rows, :] = out


def _ffn2(x1, ys, ya, w):
    r = x1.shape[0]
    tm = min(FFN_TM, r)
    rows = lambda width: pl.BlockSpec((tm, width), lambda i: (i, 0))
    weight_specs = [_const_spec((D_SSM, D_MODEL)), _const_spec((D_ATT, D_MODEL)),
                    _const_spec((1, D_MODEL))] + _ffn_weight_specs()
    return pl.pallas_call(
        _ffn2_kernel,
        grid=(r // tm,),
        in_specs=[rows(D_MODEL), rows(D_SSM), rows(D_ATT)] + weight_specs,
        out_specs=rows(D_MODEL),
        out_shape=jax.ShapeDtypeStruct((r, D_MODEL), F32),
        compiler_params=_params(1),
        name="ffn2",
    )(x1, ys, ya, *w)


def _s5_prep_kernel(lr_ref, li_ref, ls_ref, br_ref, bi_ref, cr_ref, ci_ref,
                    are_ref, aim_ref, bb_ref, cc_ref):
    lr = lr_ref[...]
    li = li_ref[...]
    dt = jnp.exp(ls_ref[...])
    mag = jnp.exp(lr * dt)
    a_re = mag * jnp.cos(li * dt)
    a_im = mag * jnp.sin(li * dt)
    den = lr * lr + li * li
    n_re = a_re - 1.0
    z_re = (n_re * lr + a_im * li) / den
    z_im = (a_im * lr - n_re * li) / den
    are_ref[...] = a_re
    aim_ref[...] = a_im

    row_g = lax.shift_right_logical(lax.broadcasted_iota(jnp.int32, (D_SSM, GP), 0), 4)
    col_g = lax.shift_right_logical(lax.broadcasted_iota(jnp.int32, (D_SSM, GP), 1), 6)
    same = row_g == col_g
    br = br_ref[...]
    bi = bi_ref[...]
    bb_ref[:, :GP] = jnp.where(same, z_re * br - z_im * bi, 0.0).astype(BF16)
    bb_ref[:, GP:] = jnp.where(same, z_re * bi + z_im * br, 0.0).astype(BF16)

    row_g = lax.shift_right_logical(lax.broadcasted_iota(jnp.int32, (GP, D_SSM), 0), 6)
    col_g = lax.shift_right_logical(lax.broadcasted_iota(jnp.int32, (GP, D_SSM), 1), 4)
    same = row_g == col_g
    cc_ref[:GP, :] = jnp.where(same, cr_ref[...], 0.0).astype(BF16)
    cc_ref[GP:, :] = jnp.where(same, -ci_ref[...], 0.0).astype(BF16)


def _s5_prep(lam_re, lam_im, log_step, b_re, b_im, c_re, c_im):
    lr = lam_re.reshape(1, GP)
    li = lam_im.reshape(1, GP)
    ls = jnp.repeat(log_step, SSM_STATE).reshape(1, GP)
    br = jnp.tile(b_re.transpose(2, 0, 1).reshape(SSM_GROUP, GP), (N_GROUPS, 1))
    bi = jnp.tile(b_im.transpose(2, 0, 1).reshape(SSM_GROUP, GP), (N_GROUPS, 1))
    cr = jnp.tile(c_re.transpose(0, 2, 1).reshape(GP, SSM_GROUP), (1, N_GROUPS))
    ci = jnp.tile(c_im.transpose(0, 2, 1).reshape(GP, SSM_GROUP), (1, N_GROUPS))
    return pl.pallas_call(
        _s5_prep_kernel,
        out_shape=(jax.ShapeDtypeStruct((1, GP), F32), jax.ShapeDtypeStruct((1, GP), F32),
                   jax.ShapeDtypeStruct((D_SSM, 2 * GP), BF16), jax.ShapeDtypeStruct((2 * GP, D_SSM), BF16)),
        compiler_params=pltpu.CompilerParams(vmem_limit_bytes=VMEM_LIMIT),
        name="s5_prep",
    )(lr, li, ls, br, bi, cr, ci)


def _scan_time_major(nb, tt):
    return tt % V7X_SUBLANES == 0 and nb % V7X_SUBLANES == 0


def _scan_pitch(tt):
    return tt if tt % 16 else tt + V7X_SUBLANES


def _mixer_kernel(nb, tt,
                  x_ref, cos_ref, sin_ref, h0_ref, pre_g, w_in, bb, are_ref, aim_ref, cc, d_skip,
                  w_glu, b_glu, ssm_g, q_g, w_uq, w_uk, kv_g,
                  ys_ref, qlat_ref, qpe_ref, kcat_ref, ckv_ref, kpe_ref, st_ref,
                  hbuf, tbuf):
    g, rpg, _ = x_ref.shape
    r = nb * tt
    time_major = _scan_time_major(nb, tt)
    pitch = _scan_pitch(tt)

    @pl.when(pl.program_id(0) == 0)
    def _():
        st_ref[...] = h0_ref[...]

    x = x_ref[...].reshape(r, D_MODEL)
    h = _rms(x, pre_g[...]).astype(BF16)
    z = _dot(h, w_in[...])
    u = z[:, :D_SSM]
    hq = z[:, D_SSM:D_SSM + Q_LORA]
    hkv = z[:, D_SSM + Q_LORA:D_SSM + Q_LORA + KV_LORA]
    hpe = z[:, D_SSM + Q_LORA + KV_LORA:]

    cos = cos_ref[...]
    sin = sin_ref[...]

    def rope(x1, x2):
        x1 = x1.reshape(g, rpg, V7X_LANES)
        x2 = x2.reshape(g, rpg, V7X_LANES)
        return x1 * cos - x2 * sin, x2 * cos + x1 * sin

    q = _dot(_rms(hq, q_g[...]).astype(BF16), w_uq[...]) * SOFTMAX_SCALE
    q_nope = q[:, :N_HEADS * QK_NOPE]
    q1, q2 = rope(q[:, N_HEADS * QK_NOPE:N_HEADS * QK_NOPE + V7X_LANES],
                  q[:, N_HEADS * QK_NOPE + V7X_LANES:])
    qpe_ref[:, :, :V7X_LANES] = q1.astype(BF16)
    qpe_ref[:, :, V7X_LANES:] = q2.astype(BF16)
    q_nope = q_nope.astype(BF16)
    for hd in range(N_HEADS):
        k0 = (hd * QK_NOPE // V7X_LANES) * V7X_LANES
        cols = slice(hd * KV_LORA, (hd + 1) * KV_LORA)
        qlat_ref[:, :, cols] = _dot(q_nope[:, k0:k0 + V7X_LANES], w_uk[k0:k0 + V7X_LANES, cols]).astype(
            BF16).reshape(g, rpg, KV_LORA)

    ckv = _rms(hkv, kv_g[...]).reshape(g, rpg, KV_LORA)
    ckv_ref[...] = ckv
    k1, k2 = rope(hpe[:, :V7X_LANES], hpe[:, V7X_LANES:])
    kcat_ref[:, :, :KV_LORA] = ckv.astype(BF16)
    kcat_ref[:, :, KV_LORA:KV_LORA + V7X_LANES] = k1.astype(BF16)
    kcat_ref[:, :, KV_LORA + V7X_LANES:] = k2.astype(BF16)
    kpe_ref[:, :, :HALF_ROPE] = k1[:, :, :HALF_ROPE]
    kpe_ref[:, :, HALF_ROPE:] = k2[:, :, :HALF_ROPE]

    n_lane_tiles = D_SSM // V7X_LANES
    if time_major:
        for c in range(n_lane_tiles):
            for b in range(nb):
                tbuf[c, b * pitch:b * pitch + tt, :] = u[b * tt:(b + 1) * tt, c * V7X_LANES:(c + 1) * V7X_LANES]
        u_rows = jnp.concatenate(
            [jnp.concatenate([tbuf[c, pl.ds(t, nb, stride=pitch), :] for t in range(tt)], axis=0)
             for c in range(n_lane_tiles)], axis=1)

        def step_rows(t, sg):
            return pl.ds(pl.multiple_of(t * nb + sg * V7X_SUBLANES, V7X_SUBLANES), V7X_SUBLANES)
    else:
        u_rows = u

        def step_rows(t, sg):
            return pl.ds(sg * (V7X_SUBLANES * tt) + t, V7X_SUBLANES, stride=tt)

    u_bf = u_rows.astype(BF16)
    tiles_per_part = GP // MXU_N
    for j in range(2 * tiles_per_part):
        k0 = ((j % tiles_per_part) // 2) * V7X_LANES
        bu = _dot(u_bf[:, k0:k0 + V7X_LANES], bb[k0:k0 + V7X_LANES, j * MXU_N:(j + 1) * MXU_N])
        for half in range(MXU_N // V7X_LANES):
            hbuf[j * (MXU_N // V7X_LANES) + half] = bu[:, half * V7X_LANES:(half + 1) * V7X_LANES]

    n_sub = nb // V7X_SUBLANES
    for lc in range(N_SLABS // SCAN_SLABS):
        slabs = [lc * SCAN_SLABS + k for k in range(SCAN_SLABS)]
        a_re = [are_ref[:, s * V7X_LANES:(s + 1) * V7X_LANES] for s in slabs]
        a_im = [aim_ref[:, s * V7X_LANES:(s + 1) * V7X_LANES] for s in slabs]

        def sub_body(sg, carry, slabs=slabs, a_re=a_re, a_im=a_im):
            seqs = pl.ds(pl.multiple_of(sg * V7X_SUBLANES, V7X_SUBLANES), V7X_SUBLANES)

            def t_body(t, state):
                rows = step_rows(t, sg)
                new = []
                for k, s in enumerate(slabs):
                    s_re, s_im = state[2 * k], state[2 * k + 1]
                    n_re = a_re[k] * s_re - a_im[k] * s_im + hbuf[s, rows, :]
                    n_im = a_re[k] * s_im + a_im[k] * s_re + hbuf[N_SLABS + s, rows, :]
                    hbuf[s, rows, :] = n_re
                    hbuf[N_SLABS + s, rows, :] = n_im
                    new += [n_re, n_im]
                return tuple(new)

            init = []
            for s in slabs:
                init += [st_ref[seqs, s * V7X_LANES:(s + 1) * V7X_LANES],
                         st_ref[seqs, GP + s * V7X_LANES:GP + (s + 1) * V7X_LANES]]
            fin = lax.fori_loop(0, tt, t_body, tuple(init), unroll=min(tt, SCAN_UNROLL))
            for k, s in enumerate(slabs):
                st_ref[seqs, s * V7X_LANES:(s + 1) * V7X_LANES] = fin[2 * k]
                st_ref[seqs, GP + s * V7X_LANES:GP + (s + 1) * V7X_LANES] = fin[2 * k + 1]
            return carry

        lax.fori_loop(0, n_sub, sub_body, 0)

    y_tiles = []
    k_per_tile = GP * MXU_N // D_SSM
    slabs_per_tile = k_per_tile // V7X_LANES
    for n in range(D_SSM // MXU_N):
        cols = slice(n * MXU_N, (n + 1) * MXU_N)
        acc = None
        for part in range(2):
            s0 = part * N_SLABS + n * slabs_per_tile
            hs = jnp.concatenate([hbuf[s0 + c] for c in range(slabs_per_tile)], axis=1).astype(BF16)
            k0 = part * GP + n * k_per_tile
            term = _dot(hs, cc[k0:k0 + k_per_tile, cols])
            acc = term if acc is None else acc + term
        y_tiles.append(acc)
    y = jnp.concatenate(y_tiles, axis=1)
    if time_major:
        for c in range(n_lane_tiles):
            tbuf[c, 0:r, :] = y[:, c * V7X_LANES:(c + 1) * V7X_LANES]
        y = jnp.concatenate(
            [jnp.concatenate([tbuf[c, pl.ds(b, tt, stride=nb), :] for b in range(nb)], axis=0)
             for c in range(n_lane_tiles)], axis=1)
    y = y + d_skip[...] * u
    y = jax.nn.gelu(y, approximate=True)
    y = y * jax.nn.sigmoid(_dot(y.astype(BF16), w_glu[...]) + b_glu[...])
    ys_ref[...] = _rms(y, ssm_g[...]).astype(BF16).reshape(g, rpg, D_SSM)


def _mixer(x1, cos, sin, h0, w, nb, tt):
    n_g, length, _ = x1.shape
    rpg = nb * tt // n_g
    assert length % rpg == 0 and rpg % 16 == 0 and nb % V7X_SUBLANES == 0
    tbuf_rows = nb * _scan_pitch(tt) if _scan_time_major(nb, tt) else V7X_SUBLANES
    row = lambda width: pl.BlockSpec((n_g, rpg, width), lambda i: (0, i, 0))
    out = lambda width, dtype: jax.ShapeDtypeStruct((n_g, length, width), dtype)
    weight_specs = [
        _const_spec((1, D_MODEL)), _const_spec((D_MODEL, D_IN_PACKED)), _const_spec((D_SSM, 2 * GP)),
        _const_spec((1, GP)), _const_spec((1, GP)), _const_spec((2 * GP, D_SSM)), _const_spec((1, D_SSM)),
        _const_spec((D_SSM, D_SSM)), _const_spec((1, D_SSM)), _const_spec((1, D_SSM)),
        _const_spec((1, Q_LORA)), _const_spec((Q_LORA, D_Q_PACKED)),
        _const_spec((N_HEADS * QK_NOPE, N_HEADS * KV_LORA)), _const_spec((1, KV_LORA)),
    ]
    rope_spec = pl.BlockSpec((rpg, V7X_LANES), lambda i: (i, 0))
    return pl.pallas_call(
        functools.partial(_mixer_kernel, nb, tt),
        grid=(length // rpg,),
        in_specs=[row(D_MODEL), rope_spec, rope_spec, _const_spec((nb, 2 * GP))] + weight_specs,
        out_specs=[row(D_SSM), row(N_HEADS * KV_LORA), row(PE_W), row(KV_LORA + PE_W), row(KV_LORA),
                   row(QK_ROPE), pl.BlockSpec((nb, 2 * GP), lambda i: (0, 0))],
        out_shape=[out(D_SSM, BF16), out(N_HEADS * KV_LORA, BF16), out(PE_W, BF16),
                   out(KV_LORA + PE_W, BF16), out(KV_LORA, F32), out(QK_ROPE, F32),
                   jax.ShapeDtypeStruct((nb, 2 * GP), F32)],
        scratch_shapes=[pltpu.VMEM((2 * N_SLABS, nb * tt, V7X_LANES), F32),
                        pltpu.VMEM((D_SSM // V7X_LANES, tbuf_rows, V7X_LANES), F32)],
        compiler_params=_params(1),
        name="mixer",
    )(x1, cos, sin, h0, *w)


def _uv_project(o_heads, w_uv):
    heads_per_tile = MXU_N // V_DIM
    tiles = []
    for n in range(D_ATT // MXU_N):
        h0 = n * heads_per_tile
        lhs = jnp.concatenate(o_heads[h0:h0 + heads_per_tile], axis=-1)
        tiles.append(_dot(lhs, w_uv[h0 * KV_LORA:(h0 + heads_per_tile) * KV_LORA,
                                    n * MXU_N:(n + 1) * MXU_N]))
    return jnp.concatenate(tiles, axis=-1)


def _prompt_attn_kernel(tq, qlat_ref, qpe_ref, kcat_ref, w_uv, att_g, ya_ref, acc_ref):
    i = pl.program_id(1)
    n_q_blocks = kcat_ref.shape[0] // tq
    qpe = qpe_ref[...]
    head_of_lane = lax.shift_right_logical(
        lax.broadcasted_iota(jnp.int32, (tq, PE_W), 1) & (V7X_LANES - 1), 4)
    q_heads = []
    for hd in range(N_HEADS):
        q_heads.append(jnp.concatenate(
            [qlat_ref[:, hd * KV_LORA:(hd + 1) * KV_LORA],
             jnp.where(head_of_lane == hd, qpe, jnp.zeros_like(qpe))], axis=-1))
    hpc = N_HEADS // ATT_ROW_CHUNKS
    rows = hpc * tq
    qs = [jnp.concatenate(q_heads[c * hpc:(c + 1) * hpc], axis=0) for c in range(ATT_ROW_CHUNKS)]

    def blocks(key_blocks, state):
        ks = [kcat_ref[j * tq:(j + 1) * tq, :] for j, _ in key_blocks]
        scores = [[_dot_nt(q, k) for k in ks] for q in qs]
        new_state = []
        for c in range(ATT_ROW_CHUNKS):
            ss = []
            for s, (_, diagonal) in zip(scores[c], key_blocks):
                if diagonal:
                    q_pos = lax.broadcasted_iota(jnp.int32, (rows, tq), 0) & (tq - 1)
                    k_pos = lax.broadcasted_iota(jnp.int32, (rows, tq), 1)
                    s = jnp.where(k_pos <= q_pos, s, -jnp.inf)
                ss.append(s)
            m_new = None
            for s in ss:
                m_s = jnp.max(s, axis=-1, keepdims=True)
                m_new = m_s if m_new is None else jnp.maximum(m_new, m_s)
            if state is not None:
                m, l = state[2 * c], state[2 * c + 1]
                m_new = jnp.maximum(m, m_new)
                alpha = jnp.exp(m - m_new)
            l_new = None
            pv = None
            for s, k in zip(ss, ks):
                p = jnp.exp(s - m_new)
                l_p = jnp.sum(p, axis=-1, keepdims=True)
                l_new = l_p if l_new is None else l_new + l_p
                term = _dot(p.astype(BF16), k[:, :KV_LORA])
                pv = term if pv is None else pv + term
            acc_rows = pl.ds(c * rows, rows)
            if state is None:
                acc_ref[acc_rows, :] = pv
            else:
                l_new = alpha * l + l_new
                acc_ref[acc_rows, :] = alpha * acc_ref[acc_rows, :] + pv
            new_state += [m_new, l_new]
        return tuple(new_state)

    def finish(state):
        l_all = jnp.concatenate([state[2 * c + 1] for c in range(ATT_ROW_CHUNKS)], axis=0)
        o = (acc_ref[...] / l_all).astype(BF16)
        o_heads = [o[hd * tq:(hd + 1) * tq, :] for hd in range(N_HEADS)]
        ya_ref[...] = _rms(_uv_project(o_heads, w_uv), att_g[...]).astype(BF16)

    for n_below in range(n_q_blocks):
        @pl.when(i == n_below)
        def _(n_below=n_below):
            state = None
            for j in range(0, n_below - n_below % 2, 2):
                state = blocks([(j, False), (j + 1, False)], state)
            if n_below % 2:
                state = blocks([(n_below - 1, False), (n_below, True)], state)
            else:
                state = blocks([(n_below, True)], state)
            finish(state)


def _prompt_attn(qlat, qpe, kcat, w_uv, att_g):
    bsz, t, _ = qlat.shape
    tq = min(ATT_TQ, t)
    assert t % tq == 0 and tq & (tq - 1) == 0
    return pl.pallas_call(
        functools.partial(_prompt_attn_kernel, tq),
        grid=(bsz, t // tq),
        in_specs=[pl.BlockSpec((None, tq, N_HEADS * KV_LORA), lambda b, i: (b, i, 0)),
                  pl.BlockSpec((None, tq, PE_W), lambda b, i: (b, i, 0)),
                  pl.BlockSpec((None, t, KV_LORA + PE_W), lambda b, i: (b, 0, 0)),
                  _const_spec((N_HEADS * KV_LORA, D_ATT)), _const_spec((1, D_ATT))],
        out_specs=pl.BlockSpec((None, tq, D_ATT), lambda b, i: (b, i, 0)),
        out_shape=jax.ShapeDtypeStruct((bsz, t, D_ATT), BF16),
        scratch_shapes=[pltpu.VMEM((N_HEADS * tq, KV_LORA), F32)],
        compiler_params=_params(2),
        name="prompt_attn",
    )(qlat, qpe, kcat, w_uv, att_g)


def _merge(m, l, acc, m_c, l_c, acc_c):
    m_new = jnp.maximum(m, m_c)
    w = jnp.exp(m - m_new)
    w_c = jnp.exp(m_c - m_new)
    return m_new, w * l + w_c * l_c, w * acc + w_c * acc_c


def _new_token_state(qlat, qpe, cnew_ref, penew_ref, seq, dec_t):
    rows = qlat.shape[0]
    qlat_f = qlat.astype(F32)
    qpe_f = qpe.astype(F32)
    t_of_row = lax.broadcasted_iota(jnp.int32, (rows, 1), 0) & (dec_t - 1)
    s_new = []
    for j in range(dec_t):
        sj = (jnp.sum(qlat_f * cnew_ref[seq, j:j + 1, :], axis=-1, keepdims=True)
              + jnp.sum(qpe_f * penew_ref[seq, j:j + 1, :], axis=-1, keepdims=True))
        s_new.append(jnp.where(t_of_row >= j, sj, -jnp.inf))
    m = s_new[0]
    for j in range(1, dec_t):
        m = jnp.maximum(m, s_new[j])
    l = jnp.zeros((rows, 1), F32)
    acc = jnp.zeros((rows, KV_LORA), F32)
    for j in range(dec_t):
        pj = jnp.exp(s_new[j] - m)
        l = l + pj
        acc = acc + pj * cnew_ref[seq, j:j + 1, :]
    return m, l, acc


def _decode_attn_kernel(n_regions, dec_t,
                        pt_ref, qlat_ref, qpe_ref, cnew_ref, penew_ref, ckv_hbm, kpe_hbm,
                        o_ref, kvbuf, pebuf, sem):
    step = pl.program_id(0)
    n_steps = pl.num_programs(0)
    n_slots = DEC_SEQS_PER_STEP * n_regions

    def page_copies(st, slot, p):
        seq = st * DEC_SEQS_PER_STEP + slot // n_regions
        page = pt_ref[seq, (slot % n_regions) * DEC_REGION_PAGES + p]
        return (pltpu.make_async_copy(ckv_hbm.at[0, page], kvbuf.at[slot, p], sem.at[0, slot]),
                pltpu.make_async_copy(kpe_hbm.at[0, page], pebuf.at[slot, p], sem.at[1, slot]))

    def start_region(st, slot):
        for p in range(DEC_REGION_PAGES):
            for cp in page_copies(st, slot, p):
                cp.start()

    def wait_region(st, slot):
        for p in range(DEC_REGION_PAGES):
            for cp in page_copies(st, slot, p):
                cp.wait()

    @pl.when(step == 0)
    def _():
        for slot in range(DEC_AHEAD):
            start_region(0, slot)

    sub_keys = DEC_SUB_PAGES * V7X_LANES
    for slot in range(n_slots):
        seq = slot // n_regions
        if slot % n_regions == 0:
            qlat = qlat_ref[seq]
            qpe = qpe_ref[seq]
            m, l, acc = _new_token_state(qlat, qpe, cnew_ref, penew_ref, seq, dec_t)
        ahead = slot + DEC_AHEAD
        if ahead < n_slots:
            start_region(step, ahead)
        else:
            @pl.when(step + 1 < n_steps)
            def _(ahead=ahead):
                start_region(step + 1, ahead - n_slots)
        wait_region(step, slot)
        kvs, scores = [], []
        for sub in range(DEC_REGION_PAGES // DEC_SUB_PAGES):
            p0 = sub * DEC_SUB_PAGES
            kv = kvbuf[slot, p0:p0 + DEC_SUB_PAGES].reshape(sub_keys, KV_LORA).astype(BF16)
            pe_t = jnp.concatenate([pebuf[slot, p0 + p] for p in range(DEC_SUB_PAGES)],
                                   axis=1).astype(BF16)
            kvs.append(kv)
            scores.append(_dot_nt(qlat, kv) + _dot(qpe, pe_t))
        for kv, s in zip(kvs, scores):
            m_c = jnp.max(s, axis=-1, keepdims=True)
            p = jnp.exp(s - m_c)
            l_c = jnp.sum(p, axis=-1, keepdims=True)
            acc_c = _dot(p.astype(BF16), kv)
            m, l, acc = _merge(m, l, acc, m_c, l_c, acc_c)
        if slot % n_regions == n_regions - 1:
            o_ref[seq] = acc / l


def _decode_attn(page_table, qlat, qpe, cnew, penew, cache_ckv, cache_kpe_t):
    bsz, n_pages = page_table.shape
    dec_t = cnew.shape[1]
    page = cache_ckv.shape[2]
    assert page == V7X_LANES and n_pages % DEC_REGION_PAGES == 0 and dec_t & (dec_t - 1) == 0
    assert bsz % DEC_SEQS_PER_STEP == 0
    rows = N_HEADS * dec_t
    n_regions = n_pages // DEC_REGION_PAGES
    n_slots = DEC_SEQS_PER_STEP * n_regions
    assert n_slots > DEC_AHEAD, "a slot must not be refilled while it is still being read"
    per_step = lambda r, width: pl.BlockSpec((DEC_SEQS_PER_STEP, r, width), lambda s, pt: (s, 0, 0))
    grid_spec = pltpu.PrefetchScalarGridSpec(
        num_scalar_prefetch=1,
        grid=(bsz // DEC_SEQS_PER_STEP,),
        in_specs=[per_step(rows, KV_LORA), per_step(rows, QK_ROPE), per_step(dec_t, KV_LORA),
                  per_step(dec_t, QK_ROPE),
                  pl.BlockSpec(memory_space=pl.ANY), pl.BlockSpec(memory_space=pl.ANY)],
        out_specs=per_step(rows, KV_LORA),
        scratch_shapes=[pltpu.VMEM((n_slots, DEC_REGION_PAGES, page, KV_LORA), F32),
                        pltpu.VMEM((n_slots, DEC_REGION_PAGES, QK_ROPE, page), F32),
                        pltpu.SemaphoreType.DMA((2, n_slots))],
    )
    return pl.pallas_call(
        functools.partial(_decode_attn_kernel, n_regions, dec_t),
        grid_spec=grid_spec,
        out_shape=jax.ShapeDtypeStruct((bsz, rows, KV_LORA), F32),
        compiler_params=_params(1),
        name="decode_attn",
    )(page_table, qlat, qpe, cnew, penew, cache_ckv, cache_kpe_t)


def _uv_kernel(o_ref, w_uv, att_g, ya_ref):
    o_heads = [o_ref[:, hd * KV_LORA:(hd + 1) * KV_LORA].astype(BF16) for hd in range(N_HEADS)]
    ya_ref[...] = _rms(_uv_project(o_heads, w_uv), att_g[...]).astype(BF16)


def _uv_proj(o_cat, w_uv, att_g):
    r = o_cat.shape[0]
    return pl.pallas_call(
        _uv_kernel,
        out_shape=jax.ShapeDtypeStruct((r, D_ATT), BF16),
        compiler_params=pltpu.CompilerParams(vmem_limit_bytes=VMEM_LIMIT),
        name="uv_proj",
    )(o_cat, w_uv, att_g)


def _rope_tables(pos):
    inv = ROPE_THETA ** (-jnp.arange(0, QK_ROPE, 2, dtype=F32) / QK_ROPE)
    ang = pos[:, None] * inv[None, :]
    return jnp.tile(jnp.cos(ang), (1, N_HEADS)), jnp.tile(jnp.sin(ang), (1, N_HEADS))


def kernel(x_prompt, x_sample, cache_ckv, cache_kpe, state_ssm_re, state_ssm_im, page_table, ffn1_pre_g, ffn1_w_gate, ffn1_w_up, ffn1_w_down, ffn1_post_g, mix_pre_g, w_in, ssm_lam_re, ssm_lam_im, ssm_log_step, ssm_b_re, ssm_b_im, ssm_c_re, ssm_c_im, ssm_d, ssm_w_glu, ssm_b_glu, q_norm_g, w_uq, kv_norm_g, w_uk, w_uv, ssm_out_g, att_out_g, w_out, mix_post_g, ffn2_pre_g, ffn2_w_gate, ffn2_w_up, ffn2_w_down, ffn2_post_g):
    assert ffn1_pre_g.shape[0] == 1, "single-layer trunk"
    bsz, seq, _ = x_prompt.shape
    dbs, dseq, _ = x_sample.shape
    past_len = page_table.shape[1] * cache_ckv.shape[2]
    assert bsz == V7X_SUBLANES

    row = lambda v: v[0].reshape(1, -1)
    ffn1_w = (row(ffn1_pre_g), ffn1_w_gate[0].astype(BF16), ffn1_w_up[0].astype(BF16),
              ffn1_w_down[0].astype(BF16), row(ffn1_post_g))
    ffn2_w = (w_out[0, :D_SSM].astype(BF16), w_out[0, D_SSM:].astype(BF16), row(mix_post_g),
              row(ffn2_pre_g), ffn2_w_gate[0].astype(BF16), ffn2_w_up[0].astype(BF16),
              ffn2_w_down[0].astype(BF16), row(ffn2_post_g))
    off_pe = D_SSM + Q_LORA + KV_LORA
    w_in_p = jnp.concatenate(
        [w_in[0, :, :off_pe],
         jnp.tile(w_in[0, :, off_pe:off_pe + HALF_ROPE], (1, N_HEADS)),
         jnp.tile(w_in[0, :, off_pe + HALF_ROPE:], (1, N_HEADS))], axis=1).astype(BF16)
    wq = w_uq[0]
    w_uq_p = jnp.concatenate(
        [wq[:, :, :QK_NOPE].reshape(Q_LORA, N_HEADS * QK_NOPE),
         wq[:, :, QK_NOPE:QK_NOPE + HALF_ROPE].reshape(Q_LORA, N_HEADS * HALF_ROPE),
         wq[:, :, QK_NOPE + HALF_ROPE:].reshape(Q_LORA, N_HEADS * HALF_ROPE)], axis=1).astype(BF16)
    eye_h = jnp.eye(N_HEADS, dtype=F32)
    w_uk_bd = jnp.einsum('hnr,hg->hngr', w_uk[0].transpose(1, 2, 0), eye_h).reshape(
        N_HEADS * QK_NOPE, N_HEADS * KV_LORA).astype(BF16)
    w_uv_bd = jnp.einsum('hrv,hg->hrgv', w_uv[0].transpose(1, 0, 2), eye_h).reshape(
        N_HEADS * KV_LORA, D_ATT).astype(BF16)
    a_re, a_im, bb, cc = _s5_prep(ssm_lam_re[0], ssm_lam_im[0], ssm_log_step[0],
                                  ssm_b_re[0], ssm_b_im[0], ssm_c_re[0], ssm_c_im[0])
    mix_w = (row(mix_pre_g), w_in_p, bb, a_re, a_im, cc, row(ssm_d), ssm_w_glu[0].astype(BF16),
             row(ssm_b_glu), row(ssm_out_g), row(q_norm_g), w_uq_p, w_uk_bd, row(kv_norm_g))
    att_g = row(att_out_g)

    cos_p, sin_p = _rope_tables(jnp.arange(seq, dtype=F32))
    x1_p = _ffn1(x_prompt.reshape(bsz * seq, D_MODEL), ffn1_w)
    ys_p, qlat_p, qpe_p, kcat_p, ckv_p, kpe_p, st_p = _mixer(
        x1_p.reshape(bsz, seq, D_MODEL), cos_p, sin_p, jnp.zeros((bsz, 2 * GP), F32), mix_w,
        bsz, min(MIX_TT, seq))
    ya_p = _prompt_attn(qlat_p, qpe_p, kcat_p, w_uv_bd, att_g)
    y_p = _ffn2(x1_p, ys_p.reshape(bsz * seq, D_SSM), ya_p.reshape(bsz * seq, D_ATT), ffn2_w)

    n_s = dbs * dseq
    cos_s, sin_s = _rope_tables(past_len + jnp.arange(dseq, dtype=F32))
    x1_s = _ffn1(x_sample.reshape(n_s, D_MODEL), ffn1_w)
    h0_s = jnp.concatenate([state_ssm_re[0].reshape(dbs, GP), state_ssm_im[0].reshape(dbs, GP)], axis=1)
    ys_s, qlat_s, qpe_s, _, ckv_s, kpe_s, st_s = _mixer(
        x1_s.reshape(1, n_s, D_MODEL), jnp.tile(cos_s, (dbs, 1)), jnp.tile(sin_s, (dbs, 1)), h0_s, mix_w,
        dbs, dseq)
    qlat_d = qlat_s.reshape(dbs, dseq, N_HEADS, KV_LORA).transpose(0, 2, 1, 3).reshape(
        dbs, N_HEADS * dseq, KV_LORA)
    qpe_d = qpe_s.reshape(dbs, dseq, 2, N_HEADS, HALF_ROPE).transpose(0, 3, 1, 2, 4).reshape(
        dbs, N_HEADS * dseq, QK_ROPE)
    cnew = ckv_s.reshape(dbs, dseq, KV_LORA)
    penew = kpe_s.reshape(dbs, dseq, QK_ROPE)
    o_d = _decode_attn(page_table, qlat_d, qpe_d, cnew, penew, cache_ckv, cache_kpe.transpose(0, 1, 3, 2))
    o_cat = o_d.reshape(dbs, N_HEADS, dseq, KV_LORA).transpose(0, 2, 1, 3).reshape(n_s, N_HEADS * KV_LORA)
    ya_s = _uv_proj(o_cat, w_uv_bd, att_g)
    y_s = _ffn2(x1_s, ys_s.reshape(n_s, D_SSM), ya_s, ffn2_w)

    state = lambda st, n: (st[:, :GP].reshape(1, n, N_GROUPS, SSM_STATE),
                           st[:, GP:].reshape(1, n, N_GROUPS, SSM_STATE))
    re_p, im_p = state(st_p, bsz)
    re_s, im_s = state(st_s, dbs)
    return (y_p.reshape(bsz, seq, D_MODEL), y_s.reshape(dbs, dseq, D_MODEL),
            ckv_p[None], kpe_p[None], re_p, im_p,
            cnew[None], penew[None], re_s, im_s)
```

```python
import functools
import math

import jax
import jax.numpy as jnp
from jax import lax
from jax.experimental import pallas as pl
from jax.experimental.pallas import tpu as pltpu

F32 = jnp.float32
BF16 = jnp.bfloat16

D_MODEL = 1024
D_FF = 2816
EPS = 1e-6
D_SSM = 512
SSM_GROUP = 16
N_GROUPS = D_SSM // SSM_GROUP
SSM_STATE = 64
GP = N_GROUPS * SSM_STATE
N_HEADS = 8
QK_NOPE = 64
QK_ROPE = 32
HALF_ROPE = QK_ROPE // 2
V_DIM = 64
D_ATT = N_HEADS * V_DIM
Q_LORA = 256
KV_LORA = 256
ROPE_THETA = 10000.0
SOFTMAX_SCALE = 1.0 / math.sqrt(QK_NOPE + QK_ROPE)
D_IN_PACKED = D_SSM + Q_LORA + KV_LORA + 2 * N_HEADS * HALF_ROPE
D_Q_PACKED = N_HEADS * QK_NOPE + 2 * N_HEADS * HALF_ROPE
PE_W = 2 * N_HEADS * HALF_ROPE

V7X_SUBLANES = 8
V7X_LANES = 128
MXU_N = 256
VMEM_LIMIT = 56 * 1024 * 1024

FFN_TM = 512
FFN_SUB_TILES = 2
MIX_TT = 64
ATT_TQ = 256
ATT_ROW_CHUNKS = 4
DEC_REGION_PAGES = 64
DEC_SEQS_PER_STEP = 2
DEC_AHEAD = 2
DEC_SUB_PAGES = 16
SCAN_SLABS = 4
SCAN_UNROLL = 64
N_SLABS = GP // V7X_LANES


def _const_spec(shape):
    nd = len(shape)
    return pl.BlockSpec(shape, lambda *_: (0,) * nd, pipeline_mode=pl.Buffered(1))


def _rms(x, g):
    return x * lax.rsqrt(jnp.mean(x * x, axis=-1, keepdims=True) + EPS) * g


def _dot(a, b):
    return jnp.dot(a, b, preferred_element_type=F32)


def _dot_nt(a, b):
    return lax.dot_general(a, b, (((1,), (1,)), ((), ())), preferred_element_type=F32)


def _ffn(xs, pre_g, wg_ref, wu_ref, wd_ref, post_g):
    hs = [_rms(x, pre_g).astype(BF16) for x in xs]
    gates = [_dot(h, wg_ref[...]) for h in hs]
    ups = [_dot(h, wu_ref[...]) for h in hs]
    acts = [(gate * jax.nn.sigmoid(gate) * up).astype(BF16) for gate, up in zip(gates, ups)]
    fs = [_dot(act, wd_ref[...]) for act in acts]
    return [x + 0.5 * _rms(f, post_g) for x, f in zip(xs, fs)]


def _params(n_grid_axes):
    return pltpu.CompilerParams(dimension_semantics=("arbitrary",) * n_grid_axes,
                                vmem_limit_bytes=VMEM_LIMIT)


def _sub_tiles(n_rows):
    sub = n_rows // FFN_SUB_TILES if n_rows % (FFN_SUB_TILES * 16) == 0 else n_rows
    return [pl.ds(r0, sub) for r0 in range(0, n_rows, sub)]


def _ffn1_kernel(x_ref, pre_g, wg, wu, wd, post_g, o_ref):
    tiles = _sub_tiles(x_ref.shape[0])
    outs = _ffn([x_ref[rows, :] for rows in tiles], pre_g[...], wg, wu, wd, post_g[...])
    for rows, out in zip(tiles, outs):
        o_ref[rows, :] = out


def _ffn_weight_specs():
    return [_const_spec((1, D_MODEL)), _const_spec((D_MODEL, D_FF)), _const_spec((D_MODEL, D_FF)),
            _const_spec((D_FF, D_MODEL)), _const_spec((1, D_MODEL))]


def _ffn1(x, w):
    r = x.shape[0]
    tm = min(FFN_TM, r)
    rows = pl.BlockSpec((tm, D_MODEL), lambda i: (i, 0))
    return pl.pallas_call(
        _ffn1_kernel,
        grid=(r // tm,),
        in_specs=[rows] + _ffn_weight_specs(),
        out_specs=rows,
        out_shape=jax.ShapeDtypeStruct((r, D_MODEL), F32),
        compiler_params=_params(1),
        name="ffn1",
    )(x, *w)


def _ffn2_kernel(x_ref, ys_ref, ya_ref, wo_s, wo_a, mix_g, pre_g, wg, wu, wd, post_g, o_ref):
    tiles = _sub_tiles(x_ref.shape[0])
    xs = []
    for rows in tiles:
        mixed = _dot(ys_ref[rows, :], wo_s[...]) + _dot(ya_ref[rows, :], wo_a[...])
        xs.append(x_ref[rows, :] + _rms(mixed, mix_g[...]))
    outs = _ffn(xs, pre_g[...], wg, wu, wd, post_g[...])
    for rows, out in zip(tiles, outs):
        o_ref[rows, :] = out


def _ffn2(x1, ys, ya, w):
    r = x1.shape[0]
    tm = min(FFN_TM, r)
    rows = lambda width: pl.BlockSpec((tm, width), lambda i: (i, 0))
    weight_specs = [_const_spec((D_SSM, D_MODEL)), _const_spec((D_ATT, D_MODEL)),
                    _const_spec((1, D_MODEL))] + _ffn_weight_specs()
    return pl.pallas_call(
        _ffn2_kernel,
        grid=(r // tm,),
        in_specs=[rows(D_MODEL), rows(D_SSM), rows(D_ATT)] + weight_specs,
        out_specs=rows(D_MODEL),
        out_shape=jax.ShapeDtypeStruct((r, D_MODEL), F32),
        compiler_params=_params(1),
        name="ffn2",
    )(x1, ys, ya, *w)


def _s5_prep_kernel(lr_ref, li_ref, ls_ref, br_ref, bi_ref, cr_ref, ci_ref,
                    are_ref, aim_ref, bb_ref, cc_ref):
    lr = lr_ref[...]
    li = li_ref[...]
    dt = jnp.exp(ls_ref[...])
    mag = jnp.exp(lr * dt)
    a_re = mag * jnp.cos(li * dt)
    a_im = mag * jnp.sin(li * dt)
    den = lr * lr + li * li
    n_re = a_re - 1.0
    z_re = (n_re * lr + a_im * li) / den
    z_im = (a_im * lr - n_re * li) / den
    are_ref[...] = a_re
    aim_ref[...] = a_im

    row_g = lax.shift_right_logical(lax.broadcasted_iota(jnp.int32, (D_SSM, GP), 0), 4)
    col_g = lax.shift_right_logical(lax.broadcasted_iota(jnp.int32, (D_SSM, GP), 1), 6)
    same = row_g == col_g
    br = br_ref[...]
    bi = bi_ref[...]
    bb_ref[:, :GP] = jnp.where(same, z_re * br - z_im * bi, 0.0).astype(BF16)
    bb_ref[:, GP:] = jnp.where(same, z_re * bi + z_im * br, 0.0).astype(BF16)

    row_g = lax.shift_right_logical(lax.broadcasted_iota(jnp.int32, (GP, D_SSM), 0), 6)
    col_g = lax.shift_right_logical(lax.broadcasted_iota(jnp.int32, (GP, D_SSM), 1), 4)
    same = row_g == col_g
    cc_ref[:GP, :] = jnp.where(same, cr_ref[...], 0.0).astype(BF16)
    cc_ref[GP:, :] = jnp.where(same, -ci_ref[...], 0.0).astype(BF16)


def _s5_prep(lam_re, lam_im, log_step, b_re, b_im, c_re, c_im):
    lr = lam_re.reshape(1, GP)
    li = lam_im.reshape(1, GP)
    ls = jnp.repeat(log_step, SSM_STATE).reshape(1, GP)
    br = jnp.tile(b_re.transpose(2, 0, 1).reshape(SSM_GROUP, GP), (N_GROUPS, 1))
    bi = jnp.tile(b_im.transpose(2, 0, 1).reshape(SSM_GROUP, GP), (N_GROUPS, 1))
    cr = jnp.tile(c_re.transpose(0, 2, 1).reshape(GP, SSM_GROUP), (1, N_GROUPS))
    ci = jnp.tile(c_im.transpose(0, 2, 1).reshape(GP, SSM_GROUP), (1, N_GROUPS))
    return pl.pallas_call(
        _s5_prep_kernel,
        out_shape=(jax.ShapeDtypeStruct((1, GP), F32), jax.ShapeDtypeStruct((1, GP), F32),
                   jax.ShapeDtypeStruct((D_SSM, 2 * GP), BF16), jax.ShapeDtypeStruct((2 * GP, D_SSM), BF16)),
        compiler_params=pltpu.CompilerParams(vmem_limit_bytes=VMEM_LIMIT),
        name="s5_prep",
    )(lr, li, ls, br, bi, cr, ci)


def _scan_time_major(nb, tt):
    return tt % V7X_SUBLANES == 0 and nb % V7X_SUBLANES == 0


def _scan_pitch(tt):
    return tt if tt % 16 else tt + V7X_SUBLANES


def _mixer_kernel(nb, tt,
                  x_ref, cos_ref, sin_ref, h0_ref, pre_g, w_in, bb, are_ref, aim_ref, cc, d_skip,
                  w_glu, b_glu, ssm_g, q_g, w_uq, w_uk, kv_g,
                  ys_ref, qlat_ref, qpe_ref, kcat_ref, ckv_ref, kpe_ref, st_ref,
                  hbuf, tbuf):
    g, rpg, _ = x_ref.shape
    r = nb * tt
    time_major = _scan_time_major(nb, tt)
    pitch = _scan_pitch(tt)

    @pl.when(pl.program_id(0) == 0)
    def _():
        st_ref[...] = h0_ref[...]

    x = x_ref[...].reshape(r, D_MODEL)
    h = _rms(x, pre_g[...]).astype(BF16)
    z = _dot(h, w_in[...])
    u = z[:, :D_SSM]
    hq = z[:, D_SSM:D_SSM + Q_LORA]
    hkv = z[:, D_SSM + Q_LORA:D_SSM + Q_LORA + KV_LORA]
    hpe = z[:, D_SSM + Q_LORA + KV_LORA:]

    cos = cos_ref[...]
    sin = sin_ref[...]

    def rope(x1, x2):
        x1 = x1.reshape(g, rpg, V7X_LANES)
        x2 = x2.reshape(g, rpg, V7X_LANES)
        return x1 * cos - x2 * sin, x2 * cos + x1 * sin

    q = _dot(_rms(hq, q_g[...]).astype(BF16), w_uq[...]) * SOFTMAX_SCALE
    q_nope = q[:, :N_HEADS * QK_NOPE]
    q1, q2 = rope(q[:, N_HEADS * QK_NOPE:N_HEADS * QK_NOPE + V7X_LANES],
                  q[:, N_HEADS * QK_NOPE + V7X_LANES:])
    qpe_ref[:, :, :V7X_LANES] = q1.astype(BF16)
    qpe_ref[:, :, V7X_LANES:] = q2.astype(BF16)
    q_nope = q_nope.astype(BF16)
    for hd in range(N_HEADS):
        k0 = (hd * QK_NOPE // V7X_LANES) * V7X_LANES
        cols = slice(hd * KV_LORA, (hd + 1) * KV_LORA)
        qlat_ref[:, :, cols] = _dot(q_nope[:, k0:k0 + V7X_LANES], w_uk[k0:k0 + V7X_LANES, cols]).astype(
            BF16).reshape(g, rpg, KV_LORA)

    ckv = _rms(hkv, kv_g[...]).reshape(g, rpg, KV_LORA)
    ckv_ref[...] = ckv
    k1, k2 = rope(hpe[:, :V7X_LANES], hpe[:, V7X_LANES:])
    kcat_ref[:, :, :KV_LORA] = ckv.astype(BF16)
    kcat_ref[:, :, KV_LORA:KV_LORA + V7X_LANES] = k1.astype(BF16)
    kcat_ref[:, :, KV_LORA + V7X_LANES:] = k2.astype(BF16)
    kpe_ref[:, :, :HALF_ROPE] = k1[:, :, :HALF_ROPE]
    kpe_ref[:, :, HALF_ROPE:] = k2[:, :, :HALF_ROPE]

    n_lane_tiles = D_SSM // V7X_LANES
    if time_major:
        for c in range(n_lane_tiles):
            for b in range(nb):
                tbuf[c, b * pitch:b * pitch + tt, :] = u[b * tt:(b + 1) * tt, c * V7X_LANES:(c + 1) * V7X_LANES]
        u_rows = jnp.concatenate(
            [jnp.concatenate([tbuf[c, pl.ds(t, nb, stride=pitch), :] for t in range(tt)], axis=0)
             for c in range(n_lane_tiles)], axis=1)

        def step_rows(t, sg):
            return pl.ds(pl.multiple_of(t * nb + sg * V7X_SUBLANES, V7X_SUBLANES), V7X_SUBLANES)
    else:
        u_rows = u

        def step_rows(t, sg):
            return pl.ds(sg * (V7X_SUBLANES * tt) + t, V7X_SUBLANES, stride=tt)

    u_bf = u_rows.astype(BF16)
    tiles_per_part = GP // MXU_N
    for j in range(2 * tiles_per_part):
        k0 = ((j % tiles_per_part) // 2) * V7X_LANES
        bu = _dot(u_bf[:, k0:k0 + V7X_LANES], bb[k0:k0 + V7X_LANES, j * MXU_N:(j + 1) * MXU_N])
        for half in range(MXU_N // V7X_LANES):
            hbuf[j * (MXU_N // V7X_LANES) + half] = bu[:, half * V7X_LANES:(half + 1) * V7X_LANES]

    n_sub = nb // V7X_SUBLANES
    for lc in range(N_SLABS // SCAN_SLABS):
        slabs = [lc * SCAN_SLABS + k for k in range(SCAN_SLABS)]
        a_re = [are_ref[:, s * V7X_LANES:(s + 1) * V7X_LANES] for s in slabs]
        a_im = [aim_ref[:, s * V7X_LANES:(s + 1) * V7X_LANES] for s in slabs]

        def sub_body(sg, carry, slabs=slabs, a_re=a_re, a_im=a_im):
            seqs = pl.ds(pl.multiple_of(sg * V7X_SUBLANES, V7X_SUBLANES), V7X_SUBLANES)

            def t_body(t, state):
                rows = step_rows(t, sg)
                new = []
                for k, s in enumerate(slabs):
                    s_re, s_im = state[2 * k], state[2 * k + 1]
                    n_re = a_re[k] * s_re - a_im[k] * s_im + hbuf[s, rows, :]
                    n_im = a_re[k] * s_im + a_im[k] * s_re + hbuf[N_SLABS + s, rows, :]
                    hbuf[s, rows, :] = n_re
                    hbuf[N_SLABS + s, rows, :] = n_im
                    new += [n_re, n_im]
                return tuple(new)

            init = []
            for s in slabs:
                init += [st_ref[seqs, s * V7X_LANES:(s + 1) * V7X_LANES],
                         st_ref[seqs, GP + s * V7X_LANES:GP + (s + 1) * V7X_LANES]]
            fin = lax.fori_loop(0, tt, t_body, tuple(init), unroll=min(tt, SCAN_UNROLL))
            for k, s in enumerate(slabs):
                st_ref[seqs, s * V7X_LANES:(s + 1) * V7X_LANES] = fin[2 * k]
                st_ref[seqs, GP + s * V7X_LANES:GP + (s + 1) * V7X_LANES] = fin[2 * k + 1]
            return carry

        lax.fori_loop(0, n_sub, sub_body, 0)

    y_tiles = []
    k_per_tile = GP * MXU_N // D_SSM
    slabs_per_tile = k_per_tile // V7X_LANES
    for n in range(D_SSM // MXU_N):
        cols = slice(n * MXU_N, (n + 1) * MXU_N)
        acc = None
        for part in range(2):
            s0 = part * N_SLABS + n * slabs_per_tile
            hs = jnp.concatenate([hbuf[s0 + c] for c in range(slabs_per_tile)], axis=1).astype(BF16)
            k0 = part * GP + n * k_per_tile
            term = _dot(hs, cc[k0:k0 + k_per_tile, cols])
            acc = term if acc is None else acc + term
        y_tiles.append(acc)
    y = jnp.concatenate(y_tiles, axis=1)
    if time_major:
        for c in range(n_lane_tiles):
            tbuf[c, 0:r, :] = y[:, c * V7X_LANES:(c + 1) * V7X_LANES]
        y = jnp.concatenate(
            [jnp.concatenate([tbuf[c, pl.ds(b, tt, stride=nb), :] for b in range(nb)], axis=0)
             for c in range(n_lane_tiles)], axis=1)
    y = y + d_skip[...] * u
    y = jax.nn.gelu(y, approximate=True)
    y = y * jax.nn.sigmoid(_dot(y.astype(BF16), w_glu[...]) + b_glu[...])
    ys_ref[...] = _rms(y, ssm_g[...]).astype(BF16).reshape(g, rpg, D_SSM)


def _mixer(x1, cos, sin, h0, w, nb, tt):
    n_g, length, _ = x1.shape
    rpg = nb * tt // n_g
    assert length % rpg == 0 and rpg % 16 == 0 and nb % V7X_SUBLANES == 0
    tbuf_rows = nb * _scan_pitch(tt) if _scan_time_major(nb, tt) else V7X_SUBLANES
    row = lambda width: pl.BlockSpec((n_g, rpg, width), lambda i: (0, i, 0))
    out = lambda width, dtype: jax.ShapeDtypeStruct((n_g, length, width), dtype)
    weight_specs = [
        _const_spec((1, D_MODEL)), _const_spec((D_MODEL, D_IN_PACKED)), _const_spec((D_SSM, 2 * GP)),
        _const_spec((1, GP)), _const_spec((1, GP)), _const_spec((2 * GP, D_SSM)), _const_spec((1, D_SSM)),
        _const_spec((D_SSM, D_SSM)), _const_spec((1, D_SSM)), _const_spec((1, D_SSM)),
        _const_spec((1, Q_LORA)), _const_spec((Q_LORA, D_Q_PACKED)),
        _const_spec((N_HEADS * QK_NOPE, N_HEADS * KV_LORA)), _const_spec((1, KV_LORA)),
    ]
    rope_spec = pl.BlockSpec((rpg, V7X_LANES), lambda i: (i, 0))
    return pl.pallas_call(
        functools.partial(_mixer_kernel, nb, tt),
        grid=(length // rpg,),
        in_specs=[row(D_MODEL), rope_spec, rope_spec, _const_spec((nb, 2 * GP))] + weight_specs,
        out_specs=[row(D_SSM), row(N_HEADS * KV_LORA), row(PE_W), row(KV_LORA + PE_W), row(KV_LORA),
                   row(QK_ROPE), pl.BlockSpec((nb, 2 * GP), lambda i: (0, 0))],
        out_shape=[out(D_SSM, BF16), out(N_HEADS * KV_LORA, BF16), out(PE_W, BF16),
                   out(KV_LORA + PE_W, BF16), out(KV_LORA, F32), out(QK_ROPE, F32),
                   jax.ShapeDtypeStruct((nb, 2 * GP), F32)],
        scratch_shapes=[pltpu.VMEM((2 * N_SLABS, nb * tt, V7X_LANES), F32),
                        pltpu.VMEM((D_SSM // V7X_LANES, tbuf_rows, V7X_LANES), F32)],
        compiler_params=_params(1),
        name="mixer",
    )(x1, cos, sin, h0, *w)


def _uv_project(o_heads, w_uv):
    heads_per_tile = MXU_N // V_DIM
    tiles = []
    for n in range(D_ATT // MXU_N):
        h0 = n * heads_per_tile
        lhs = jnp.concatenate(o_heads[h0:h0 + heads_per_tile], axis=-1)
        tiles.append(_dot(lhs, w_uv[h0 * KV_LORA:(h0 + heads_per_tile) * KV_LORA,
                                    n * MXU_N:(n + 1) * MXU_N]))
    return jnp.concatenate(tiles, axis=-1)


def _prompt_attn_kernel(tq, n_below, qlat_ref, qpe_ref, kcat_ref, w_uv, att_g, ya_ref, acc_ref):
    qpe = qpe_ref[...]
    head_of_lane = lax.shift_right_logical(
        lax.broadcasted_iota(jnp.int32, (tq, PE_W), 1) & (V7X_LANES - 1), 4)
    q_heads = []
    for hd in range(N_HEADS):
        q_heads.append(jnp.concatenate(
            [qlat_ref[:, hd * KV_LORA:(hd + 1) * KV_LORA],
             jnp.where(head_of_lane == hd, qpe, jnp.zeros_like(qpe))], axis=-1))
    hpc = N_HEADS // ATT_ROW_CHUNKS
    rows = hpc * tq
    qs = [jnp.concatenate(q_heads[c * hpc:(c + 1) * hpc], axis=0) for c in range(ATT_ROW_CHUNKS)]

    def blocks(key_blocks, state):
        ks = [kcat_ref[j * tq:(j + 1) * tq, :] for j, _ in key_blocks]
        scores = [[_dot_nt(q, k) for k in ks] for q in qs]
        new_state = []
        for c in range(ATT_ROW_CHUNKS):
            ss = []
            for s, (_, diagonal) in zip(scores[c], key_blocks):
                if diagonal:
                    q_pos = lax.broadcasted_iota(jnp.int32, (rows, tq), 0) & (tq - 1)
                    k_pos = lax.broadcasted_iota(jnp.int32, (rows, tq), 1)
                    s = jnp.where(k_pos <= q_pos, s, -jnp.inf)
                ss.append(s)
            m_new = None
            for s in ss:
                m_s = jnp.max(s, axis=-1, keepdims=True)
                m_new = m_s if m_new is None else jnp.maximum(m_new, m_s)
            if state is not None:
                m, l = state[2 * c], state[2 * c + 1]
                m_new = jnp.maximum(m, m_new)
                alpha = jnp.exp(m - m_new)
            l_new = None
            pv = None
            for s, k in zip(ss, ks):
                p = jnp.exp(s - m_new)
                l_p = jnp.sum(p, axis=-1, keepdims=True)
                l_new = l_p if l_new is None else l_new + l_p
                term = _dot(p.astype(BF16), k[:, :KV_LORA])
                pv = term if pv is None else pv + term
            acc_rows = pl.ds(c * rows, rows)
            if state is None:
                acc_ref[acc_rows, :] = pv
            else:
                l_new = alpha * l + l_new
                acc_ref[acc_rows, :] = alpha * acc_ref[acc_rows, :] + pv
            new_state += [m_new, l_new]
        return tuple(new_state)

    state = None
    for j in range(0, n_below - n_below % 2, 2):
        state = blocks([(j, False), (j + 1, False)], state)
    if n_below % 2:
        state = blocks([(n_below - 1, False), (n_below, True)], state)
    else:
        state = blocks([(n_below, True)], state)

    l_all = jnp.concatenate([state[2 * c + 1] for c in range(ATT_ROW_CHUNKS)], axis=0)
    o = (acc_ref[...] / l_all).astype(BF16)
    o_heads = [o[hd * tq:(hd + 1) * tq, :] for hd in range(N_HEADS)]
    ya_ref[...] = _rms(_uv_project(o_heads, w_uv), att_g[...]).astype(BF16)


def _prompt_attn(qlat, qpe, kcat, w_uv, att_g):
    bsz, t, _ = qlat.shape
    tq = min(ATT_TQ, t)
    assert t % tq == 0 and tq & (tq - 1) == 0
    outs = []
    for i in range(t // tq):
        outs.append(pl.pallas_call(
            functools.partial(_prompt_attn_kernel, tq, i),
            grid=(bsz,),
            in_specs=[pl.BlockSpec((None, tq, N_HEADS * KV_LORA), lambda b, i=i: (b, i, 0)),
                      pl.BlockSpec((None, tq, PE_W), lambda b, i=i: (b, i, 0)),
                      pl.BlockSpec((None, (i + 1) * tq, KV_LORA + PE_W), lambda b: (b, 0, 0)),
                      _const_spec((N_HEADS * KV_LORA, D_ATT)), _const_spec((1, D_ATT))],
            out_specs=pl.BlockSpec((None, tq, D_ATT), lambda b: (b, 0, 0)),
            out_shape=jax.ShapeDtypeStruct((bsz, tq, D_ATT), BF16),
            scratch_shapes=[pltpu.VMEM((N_HEADS * tq, KV_LORA), F32)],
            compiler_params=_params(1),
            name=f"prompt_attn_q{i}",
        )(qlat, qpe, kcat, w_uv, att_g))
    return jnp.concatenate(outs, axis=1)


def _merge(m, l, acc, m_c, l_c, acc_c):
    m_new = jnp.maximum(m, m_c)
    w = jnp.exp(m - m_new)
    w_c = jnp.exp(m_c - m_new)
    return m_new, w * l + w_c * l_c, w * acc + w_c * acc_c


def _new_token_state(qlat, qpe, cnew_ref, penew_ref, seq, dec_t):
    rows = qlat.shape[0]
    qlat_f = qlat.astype(F32)
    qpe_f = qpe.astype(F32)
    t_of_row = lax.broadcasted_iota(jnp.int32, (rows, 1), 0) & (dec_t - 1)
    s_new = []
    for j in range(dec_t):
        sj = (jnp.sum(qlat_f * cnew_ref[seq, j:j + 1, :], axis=-1, keepdims=True)
              + jnp.sum(qpe_f * penew_ref[seq, j:j + 1, :], axis=-1, keepdims=True))
        s_new.append(jnp.where(t_of_row >= j, sj, -jnp.inf))
    m = s_new[0]
    for j in range(1, dec_t):
        m = jnp.maximum(m, s_new[j])
    l = jnp.zeros((rows, 1), F32)
    acc = jnp.zeros((rows, KV_LORA), F32)
    for j in range(dec_t):
        pj = jnp.exp(s_new[j] - m)
        l = l + pj
        acc = acc + pj * cnew_ref[seq, j:j + 1, :]
    return m, l, acc


def _decode_attn_kernel(n_regions, dec_t,
                        pt_ref, qlat_ref, qpe_ref, cnew_ref, penew_ref, ckv_hbm, kpe_hbm,
                        o_ref, kvbuf, pebuf, sem):
    step = pl.program_id(0)
    n_steps = pl.num_programs(0)
    n_slots = DEC_SEQS_PER_STEP * n_regions

    def page_copies(st, slot, p):
        seq = st * DEC_SEQS_PER_STEP + slot // n_regions
        page = pt_ref[seq, (slot % n_regions) * DEC_REGION_PAGES + p]
        return (pltpu.make_async_copy(ckv_hbm.at[0, page], kvbuf.at[slot, p], sem.at[0, slot]),
                pltpu.make_async_copy(kpe_hbm.at[0, page], pebuf.at[slot, p], sem.at[1, slot]))

    def start_region(st, slot):
        for p in range(DEC_REGION_PAGES):
            for cp in page_copies(st, slot, p):
                cp.start()

    def wait_region(st, slot):
        for p in range(DEC_REGION_PAGES):
            for cp in page_copies(st, slot, p):
                cp.wait()

    @pl.when(step == 0)
    def _():
        for slot in range(DEC_AHEAD):
            start_region(0, slot)

    sub_keys = DEC_SUB_PAGES * V7X_LANES
    for slot in range(n_slots):
        seq = slot // n_regions
        if slot % n_regions == 0:
            qlat = qlat_ref[seq]
            qpe = qpe_ref[seq]
            m, l, acc = _new_token_state(qlat, qpe, cnew_ref, penew_ref, seq, dec_t)
        ahead = slot + DEC_AHEAD
        if ahead < n_slots:
            start_region(step, ahead)
        else:
            @pl.when(step + 1 < n_steps)
            def _(ahead=ahead):
                start_region(step + 1, ahead - n_slots)
        wait_region(step, slot)
        kvs, scores = [], []
        for sub in range(DEC_REGION_PAGES // DEC_SUB_PAGES):
            p0 = sub * DEC_SUB_PAGES
            kv = kvbuf[slot, p0:p0 + DEC_SUB_PAGES].reshape(sub_keys, KV_LORA).astype(BF16)
            pe_t = jnp.concatenate([pebuf[slot, p0 + p] for p in range(DEC_SUB_PAGES)],
                                   axis=1).astype(BF16)
            kvs.append(kv)
            scores.append(_dot_nt(qlat, kv) + _dot(qpe, pe_t))
        for kv, s in zip(kvs, scores):
            m_c = jnp.max(s, axis=-1, keepdims=True)
            p = jnp.exp(s - m_c)
            l_c = jnp.sum(p, axis=-1, keepdims=True)
            acc_c = _dot(p.astype(BF16), kv)
            m, l, acc = _merge(m, l, acc, m_c, l_c, acc_c)
        if slot % n_regions == n_regions - 1:
            o_ref[seq] = acc / l


def _decode_attn(page_table, qlat, qpe, cnew, penew, cache_ckv, cache_kpe_t):
    bsz, n_pages = page_table.shape
    dec_t = cnew.shape[1]
    page = cache_ckv.shape[2]
    assert page == V7X_LANES and n_pages % DEC_REGION_PAGES == 0 and dec_t & (dec_t - 1) == 0
    assert bsz % DEC_SEQS_PER_STEP == 0
    rows = N_HEADS * dec_t
    n_regions = n_pages // DEC_REGION_PAGES
    n_slots = DEC_SEQS_PER_STEP * n_regions
    assert n_slots > DEC_AHEAD, "a slot must not be refilled while it is still being read"
    per_step = lambda r, width: pl.BlockSpec((DEC_SEQS_PER_STEP, r, width), lambda s, pt: (s, 0, 0))
    grid_spec = pltpu.PrefetchScalarGridSpec(
        num_scalar_prefetch=1,
        grid=(bsz // DEC_SEQS_PER_STEP,),
        in_specs=[per_step(rows, KV_LORA), per_step(rows, QK_ROPE), per_step(dec_t, KV_LORA),
                  per_step(dec_t, QK_ROPE),
                  pl.BlockSpec(memory_space=pl.ANY), pl.BlockSpec(memory_space=pl.ANY)],
        out_specs=per_step(rows, KV_LORA),
        scratch_shapes=[pltpu.VMEM((n_slots, DEC_REGION_PAGES, page, KV_LORA), F32),
                        pltpu.VMEM((n_slots, DEC_REGION_PAGES, QK_ROPE, page), F32),
                        pltpu.SemaphoreType.DMA((2, n_slots))],
    )
    return pl.pallas_call(
        functools.partial(_decode_attn_kernel, n_regions, dec_t),
        grid_spec=grid_spec,
        out_shape=jax.ShapeDtypeStruct((bsz, rows, KV_LORA), F32),
        compiler_params=_params(1),
        name="decode_attn",
    )(page_table, qlat, qpe, cnew, penew, cache_ckv, cache_kpe_t)


def _uv_kernel(o_ref, w_uv, att_g, ya_ref):
    o_heads = [o_ref[:, hd * KV_LORA:(hd + 1) * KV_LORA].astype(BF16) for hd in range(N_HEADS)]
    ya_ref[...] = _rms(_uv_project(o_heads, w_uv), att_g[...]).astype(BF16)


def _uv_proj(o_cat, w_uv, att_g):
    r = o_cat.shape[0]
    return pl.pallas_call(
        _uv_kernel,
        out_shape=jax.ShapeDtypeStruct((r, D_ATT), BF16),
        compiler_params=pltpu.CompilerParams(vmem_limit_bytes=VMEM_LIMIT),
        name="uv_proj",
    )(o_cat, w_uv, att_g)


def _rope_tables(pos):
    inv = ROPE_THETA ** (-jnp.arange(0, QK_ROPE, 2, dtype=F32) / QK_ROPE)
    ang = pos[:, None] * inv[None, :]
    return jnp.tile(jnp.cos(ang), (1, N_HEADS)), jnp.tile(jnp.sin(ang), (1, N_HEADS))


def kernel(x_prompt, x_sample, cache_ckv, cache_kpe, state_ssm_re, state_ssm_im, page_table, ffn1_pre_g, ffn1_w_gate, ffn1_w_up, ffn1_w_down, ffn1_post_g, mix_pre_g, w_in, ssm_lam_re, ssm_lam_im, ssm_log_step, ssm_b_re, ssm_b_im, ssm_c_re, ssm_c_im, ssm_d, ssm_w_glu, ssm_b_glu, q_norm_g, w_uq, kv_norm_g, w_uk, w_uv, ssm_out_g, att_out_g, w_out, mix_post_g, ffn2_pre_g, ffn2_w_gate, ffn2_w_up, ffn2_w_down, ffn2_post_g):
    assert ffn1_pre_g.shape[0] == 1, "single-layer trunk"
    bsz, seq, _ = x_prompt.shape
    dbs, dseq, _ = x_sample.shape
    past_len = page_table.shape[1] * cache_ckv.shape[2]
    assert bsz == V7X_SUBLANES

    row = lambda v: v[0].reshape(1, -1)
    ffn1_w = (row(ffn1_pre_g), ffn1_w_gate[0].astype(BF16), ffn1_w_up[0].astype(BF16),
              ffn1_w_down[0].astype(BF16), row(ffn1_post_g))
    ffn2_w = (w_out[0, :D_SSM].astype(BF16), w_out[0, D_SSM:].astype(BF16), row(mix_post_g),
              row(ffn2_pre_g), ffn2_w_gate[0].astype(BF16), ffn2_w_up[0].astype(BF16),
              ffn2_w_down[0].astype(BF16), row(ffn2_post_g))
    off_pe = D_SSM + Q_LORA + KV_LORA
    w_in_p = jnp.concatenate(
        [w_in[0, :, :off_pe],
         jnp.tile(w_in[0, :, off_pe:off_pe + HALF_ROPE], (1, N_HEADS)),
         jnp.tile(w_in[0, :, off_pe + HALF_ROPE:], (1, N_HEADS))], axis=1).astype(BF16)
    wq = w_uq[0]
    w_uq_p = jnp.concatenate(
        [wq[:, :, :QK_NOPE].reshape(Q_LORA, N_HEADS * QK_NOPE),
         wq[:, :, QK_NOPE:QK_NOPE + HALF_ROPE].reshape(Q_LORA, N_HEADS * HALF_ROPE),
         wq[:, :, QK_NOPE + HALF_ROPE:].reshape(Q_LORA, N_HEADS * HALF_ROPE)], axis=1).astype(BF16)
    eye_h = jnp.eye(N_HEADS, dtype=F32)
    w_uk_bd = jnp.einsum('hnr,hg->hngr', w_uk[0].transpose(1, 2, 0), eye_h).reshape(
        N_HEADS * QK_NOPE, N_HEADS * KV_LORA).astype(BF16)
    w_uv_bd = jnp.einsum('hrv,hg->hrgv', w_uv[0].transpose(1, 0, 2), eye_h).reshape(
        N_HEADS * KV_LORA, D_ATT).astype(BF16)
    a_re, a_im, bb, cc = _s5_prep(ssm_lam_re[0], ssm_lam_im[0], ssm_log_step[0],
                                  ssm_b_re[0], ssm_b_im[0], ssm_c_re[0], ssm_c_im[0])
    mix_w = (row(mix_pre_g), w_in_p, bb, a_re, a_im, cc, row(ssm_d), ssm_w_glu[0].astype(BF16),
             row(ssm_b_glu), row(ssm_out_g), row(q_norm_g), w_uq_p, w_uk_bd, row(kv_norm_g))
    att_g = row(att_out_g)

    cos_p, sin_p = _rope_tables(jnp.arange(seq, dtype=F32))
    x1_p = _ffn1(x_prompt.reshape(bsz * seq, D_MODEL), ffn1_w)
    ys_p, qlat_p, qpe_p, kcat_p, ckv_p, kpe_p, st_p = _mixer(
        x1_p.reshape(bsz, seq, D_MODEL), cos_p, sin_p, jnp.zeros((bsz, 2 * GP), F32), mix_w,
        bsz, min(MIX_TT, seq))
    ya_p = _prompt_attn(qlat_p, qpe_p, kcat_p, w_uv_bd, att_g)
    y_p = _ffn2(x1_p, ys_p.reshape(bsz * seq, D_SSM), ya_p.reshape(bsz * seq, D_ATT), ffn2_w)

    n_s = dbs * dseq
    cos_s, sin_s = _rope_tables(past_len + jnp.arange(dseq, dtype=F32))
    x1_s = _ffn1(x_sample.reshape(n_s, D_MODEL), ffn1_w)
    h0_s = jnp.concatenate([state_ssm_re[0].reshape(dbs, GP), state_ssm_im[0].reshape(dbs, GP)], axis=1)
    ys_s, qlat_s, qpe_s, _, ckv_s, kpe_s, st_s = _mixer(
        x1_s.reshape(1, n_s, D_MODEL), jnp.tile(cos_s, (dbs, 1)), jnp.tile(sin_s, (dbs, 1)), h0_s, mix_w,
        dbs, dseq)
    qlat_d = qlat_s.reshape(dbs, dseq, N_HEADS, KV_LORA).transpose(0, 2, 1, 3).reshape(
        dbs, N_HEADS * dseq, KV_LORA)
    qpe_d = qpe_s.reshape(dbs, dseq, 2, N_HEADS, HALF_ROPE).transpose(0, 3, 1, 2, 4).reshape(
        dbs, N_HEADS * dseq, QK_ROPE)
    cnew = ckv_s.reshape(dbs, dseq, KV_LORA)
    penew = kpe_s.reshape(dbs, dseq, QK_ROPE)
    o_d = _decode_attn(page_table, qlat_d, qpe_d, cnew, penew, cache_ckv, cache_kpe.transpose(0, 1, 3, 2))
    o_cat = o_d.reshape(dbs, N_HEADS, dseq, KV_LORA).transpose(0, 2, 1, 3).reshape(n_s, N_HEADS * KV_LORA)
    ya_s = _uv_proj(o_cat, w_uv_bd, att_g)
    y_s = _ffn2(x1_s, ys_s.reshape(n_s, D_SSM), ya_s, ffn2_w)

    state = lambda st, n: (st[:, :GP].reshape(1, n, N_GROUPS, SSM_STATE),
                           st[:, GP:].reshape(1, n, N_GROUPS, SSM_STATE))
    re_p, im_p = state(st_p, bsz)
    re_s, im_s = state(st_s, dbs)
    return (y_p.reshape(bsz, seq, D_MODEL), y_s.reshape(dbs, dseq, D_MODEL),
            ckv_p[None], kpe_p[None], re_p, im_p,
            cnew[None], penew[None], re_s, im_s)
```

```python
import functools
import math

import jax
import jax.numpy as jnp
from jax import lax
from jax.experimental import pallas as pl
from jax.experimental.pallas import tpu as pltpu

F32 = jnp.float32
BF16 = jnp.bfloat16

D_MODEL = 1024
D_FF = 2816
EPS = 1e-6
D_SSM = 512
SSM_GROUP = 16
N_GROUPS = D_SSM // SSM_GROUP
SSM_STATE = 64
GP = N_GROUPS * SSM_STATE
N_HEADS = 8
QK_NOPE = 64
QK_ROPE = 32
HALF_ROPE = QK_ROPE // 2
V_DIM = 64
D_ATT = N_HEADS * V_DIM
Q_LORA = 256
KV_LORA = 256
ROPE_THETA = 10000.0
SOFTMAX_SCALE = 1.0 / math.sqrt(QK_NOPE + QK_ROPE)
D_IN_PACKED = D_SSM + Q_LORA + KV_LORA + 2 * N_HEADS * HALF_ROPE
D_Q_PACKED = N_HEADS * QK_NOPE + 2 * N_HEADS * HALF_ROPE
PE_W = 2 * N_HEADS * HALF_ROPE

V7X_SUBLANES = 8
V7X_LANES = 128
MXU_N = 256
VMEM_LIMIT = 56 * 1024 * 1024

FFN_TM = 512
FFN_SUB_TILES = 2
MIX_TT = 64
ATT_TQ = 256
ATT_ROW_CHUNKS = 8
ATT_KEY_GROUP = 8
DEC_REGION_PAGES = 64
DEC_SEQS_PER_STEP = 2
DEC_AHEAD = 2
DEC_SUB_PAGES = 16
SCAN_SLABS = 4
SCAN_UNROLL = 64
N_SLABS = GP // V7X_LANES


def _const_spec(shape):
    nd = len(shape)
    return pl.BlockSpec(shape, lambda *_: (0,) * nd, pipeline_mode=pl.Buffered(1))


def _rms(x, g):
    return x * lax.rsqrt(jnp.mean(x * x, axis=-1, keepdims=True) + EPS) * g


def _dot(a, b):
    return jnp.dot(a, b, preferred_element_type=F32)


def _dot_nt(a, b):
    return lax.dot_general(a, b, (((1,), (1,)), ((), ())), preferred_element_type=F32)


def _ffn(xs, pre_g, wg_ref, wu_ref, wd_ref, post_g):
    hs = [_rms(x, pre_g).astype(BF16) for x in xs]
    gates = [_dot(h, wg_ref[...]) for h in hs]
    ups = [_dot(h, wu_ref[...]) for h in hs]
    acts = [(gate * jax.nn.sigmoid(gate) * up).astype(BF16) for gate, up in zip(gates, ups)]
    fs = [_dot(act, wd_ref[...]) for act in acts]
    return [x + 0.5 * _rms(f, post_g) for x, f in zip(xs, fs)]


def _params(n_grid_axes):
    return pltpu.CompilerParams(dimension_semantics=("arbitrary",) * n_grid_axes,
                                vmem_limit_bytes=VMEM_LIMIT)


def _sub_tiles(n_rows):
    sub = n_rows // FFN_SUB_TILES if n_rows % (FFN_SUB_TILES * 16) == 0 else n_rows
    return [pl.ds(r0, sub) for r0 in range(0, n_rows, sub)]


def _ffn1_kernel(x_ref, pre_g, wg, wu, wd, post_g, o_ref):
    tiles = _sub_tiles(x_ref.shape[0])
    outs = _ffn([x_ref[rows, :] for rows in tiles], pre_g[...], wg, wu, wd, post_g[...])
    for rows, out in zip(tiles, outs):
        o_ref[rows, :] = out


def _ffn_weight_specs():
    return [_const_spec((1, D_MODEL)), _const_spec((D_MODEL, D_FF)), _const_spec((D_MODEL, D_FF)),
            _const_spec((D_FF, D_MODEL)), _const_spec((1, D_MODEL))]


def _ffn1(x, w):
    r = x.shape[0]
    tm = min(FFN_TM, r)
    rows = pl.BlockSpec((tm, D_MODEL), lambda i: (i, 0))
    return pl.pallas_call(
        _ffn1_kernel,
        grid=(r // tm,),
        in_specs=[rows] + _ffn_weight_specs(),
        out_specs=rows,
        out_shape=jax.ShapeDtypeStruct((r, D_MODEL), F32),
        compiler_params=_params(1),
        name="ffn1",
    )(x, *w)


def _ffn2_kernel(x_ref, ys_ref, ya_ref, wo_s, wo_a, mix_g, pre_g, wg, wu, wd, post_g, o_ref):
    tiles = _sub_tiles(x_ref.shape[0])
    xs = []
    for rows in tiles:
        mixed = _dot(ys_ref[rows, :], wo_s[...]) + _dot(ya_ref[rows, :], wo_a[...])
        xs.append(x_ref[rows, :] + _rms(mixed, mix_g[...]))
    outs = _ffn(xs, pre_g[...], wg, wu, wd, post_g[...])
    for rows, out in zip(tiles, outs):
        o_ref[rows, :] = out


def _ffn2(x1, ys, ya, w):
    r = x1.shape[0]
    tm = min(FFN_TM, r)
    rows = lambda width: pl.BlockSpec((tm, width), lambda i: (i, 0))
    weight_specs = [_const_spec((D_SSM, D_MODEL)), _const_spec((D_ATT, D_MODEL)),
                    _const_spec((1, D_MODEL))] + _ffn_weight_specs()
    return pl.pallas_call(
        _ffn2_kernel,
        grid=(r // tm,),
        in_specs=[rows(D_MODEL), rows(D_SSM), rows(D_ATT)] + weight_specs,
        out_specs=rows(D_MODEL),
        out_shape=jax.ShapeDtypeStruct((r, D_MODEL), F32),
        compiler_params=_params(1),
        name="ffn2",
    )(x1, ys, ya, *w)


def _s5_prep_kernel(lr_ref, li_ref, ls_ref, br_ref, bi_ref, cr_ref, ci_ref,
                    are_ref, aim_ref, bb_ref, cc_ref):
    lr = lr_ref[...]
    li = li_ref[...]
    dt = jnp.exp(ls_ref[...])
    mag = jnp.exp(lr * dt)
    a_re = mag * jnp.cos(li * dt)
    a_im = mag * jnp.sin(li * dt)
    den = lr * lr + li * li
    n_re = a_re - 1.0
    z_re = (n_re * lr + a_im * li) / den
    z_im = (a_im * lr - n_re * li) / den
    are_ref[...] = a_re
    aim_ref[...] = a_im

    row_g = lax.shift_right_logical(lax.broadcasted_iota(jnp.int32, (D_SSM, GP), 0), 4)
    col_g = lax.shift_right_logical(lax.broadcasted_iota(jnp.int32, (D_SSM, GP), 1), 6)
    same = row_g == col_g
    br = br_ref[...]
    bi = bi_ref[...]
    bb_ref[:, :GP] = jnp.where(same, z_re * br - z_im * bi, 0.0).astype(BF16)
    bb_ref[:, GP:] = jnp.where(same, z_re * bi + z_im * br, 0.0).astype(BF16)

    row_g = lax.shift_right_logical(lax.broadcasted_iota(jnp.int32, (GP, D_SSM), 0), 6)
    col_g = lax.shift_right_logical(lax.broadcasted_iota(jnp.int32, (GP, D_SSM), 1), 4)
    same = row_g == col_g
    cc_ref[:GP, :] = jnp.where(same, cr_ref[...], 0.0).astype(BF16)
    cc_ref[GP:, :] = jnp.where(same, -ci_ref[...], 0.0).astype(BF16)


def _s5_prep(lam_re, lam_im, log_step, b_re, b_im, c_re, c_im):
    lr = lam_re.reshape(1, GP)
    li = lam_im.reshape(1, GP)
    ls = jnp.repeat(log_step, SSM_STATE).reshape(1, GP)
    br = jnp.tile(b_re.transpose(2, 0, 1).reshape(SSM_GROUP, GP), (N_GROUPS, 1))
    bi = jnp.tile(b_im.transpose(2, 0, 1).reshape(SSM_GROUP, GP), (N_GROUPS, 1))
    cr = jnp.tile(c_re.transpose(0, 2, 1).reshape(GP, SSM_GROUP), (1, N_GROUPS))
    ci = jnp.tile(c_im.transpose(0, 2, 1).reshape(GP, SSM_GROUP), (1, N_GROUPS))
    return pl.pallas_call(
        _s5_prep_kernel,
        out_shape=(jax.ShapeDtypeStruct((1, GP), F32), jax.ShapeDtypeStruct((1, GP), F32),
                   jax.ShapeDtypeStruct((D_SSM, 2 * GP), BF16), jax.ShapeDtypeStruct((2 * GP, D_SSM), BF16)),
        compiler_params=pltpu.CompilerParams(vmem_limit_bytes=VMEM_LIMIT),
        name="s5_prep",
    )(lr, li, ls, br, bi, cr, ci)


def _scan_time_major(nb, tt):
    return tt % V7X_SUBLANES == 0 and nb % V7X_SUBLANES == 0


def _scan_pitch(tt):
    return tt if tt % 16 else tt + V7X_SUBLANES


def _mixer_kernel(nb, tt,
                  x_ref, cos_ref, sin_ref, h0_ref, pre_g, w_in, bb, are_ref, aim_ref, cc, d_skip,
                  w_glu, b_glu, ssm_g, q_g, w_uq, w_uk, kv_g,
                  ys_ref, qlat_ref, qpe_ref, kcat_ref, ckv_ref, kpe_ref, st_ref,
                  hbuf, tbuf):
    g, rpg, _ = x_ref.shape
    r = nb * tt
    time_major = _scan_time_major(nb, tt)
    pitch = _scan_pitch(tt)

    @pl.when(pl.program_id(0) == 0)
    def _():
        st_ref[...] = h0_ref[...]

    x = x_ref[...].reshape(r, D_MODEL)
    h = _rms(x, pre_g[...]).astype(BF16)
    z = _dot(h, w_in[...])
    u = z[:, :D_SSM]
    hq = z[:, D_SSM:D_SSM + Q_LORA]
    hkv = z[:, D_SSM + Q_LORA:D_SSM + Q_LORA + KV_LORA]
    hpe = z[:, D_SSM + Q_LORA + KV_LORA:]

    cos = cos_ref[...]
    sin = sin_ref[...]

    def rope(x1, x2):
        x1 = x1.reshape(g, rpg, V7X_LANES)
        x2 = x2.reshape(g, rpg, V7X_LANES)
        return x1 * cos - x2 * sin, x2 * cos + x1 * sin

    q = _dot(_rms(hq, q_g[...]).astype(BF16), w_uq[...]) * SOFTMAX_SCALE
    q_nope = q[:, :N_HEADS * QK_NOPE]
    q1, q2 = rope(q[:, N_HEADS * QK_NOPE:N_HEADS * QK_NOPE + V7X_LANES],
                  q[:, N_HEADS * QK_NOPE + V7X_LANES:])
    qpe_ref[:, :, :V7X_LANES] = q1.astype(BF16)
    qpe_ref[:, :, V7X_LANES:] = q2.astype(BF16)
    q_nope = q_nope.astype(BF16)
    for hd in range(N_HEADS):
        k0 = (hd * QK_NOPE // V7X_LANES) * V7X_LANES
        cols = slice(hd * KV_LORA, (hd + 1) * KV_LORA)
        qlat_ref[:, :, cols] = _dot(q_nope[:, k0:k0 + V7X_LANES], w_uk[k0:k0 + V7X_LANES, cols]).astype(
            BF16).reshape(g, rpg, KV_LORA)

    ckv = _rms(hkv, kv_g[...]).reshape(g, rpg, KV_LORA)
    ckv_ref[...] = ckv
    k1, k2 = rope(hpe[:, :V7X_LANES], hpe[:, V7X_LANES:])
    kcat_ref[:, :, :KV_LORA] = ckv.astype(BF16)
    kcat_ref[:, :, KV_LORA:KV_LORA + V7X_LANES] = k1.astype(BF16)
    kcat_ref[:, :, KV_LORA + V7X_LANES:] = k2.astype(BF16)
    kpe_ref[:, :, :HALF_ROPE] = k1[:, :, :HALF_ROPE]
    kpe_ref[:, :, HALF_ROPE:] = k2[:, :, :HALF_ROPE]

    n_lane_tiles = D_SSM // V7X_LANES
    if time_major:
        for c in range(n_lane_tiles):
            for b in range(nb):
                tbuf[c, b * pitch:b * pitch + tt, :] = u[b * tt:(b + 1) * tt, c * V7X_LANES:(c + 1) * V7X_LANES]
        u_rows = jnp.concatenate(
            [jnp.concatenate([tbuf[c, pl.ds(t, nb, stride=pitch), :] for t in range(tt)], axis=0)
             for c in range(n_lane_tiles)], axis=1)

        def step_rows(t, sg):
            return pl.ds(pl.multiple_of(t * nb + sg * V7X_SUBLANES, V7X_SUBLANES), V7X_SUBLANES)
    else:
        u_rows = u

        def step_rows(t, sg):
            return pl.ds(sg * (V7X_SUBLANES * tt) + t, V7X_SUBLANES, stride=tt)

    u_bf = u_rows.astype(BF16)
    tiles_per_part = GP // MXU_N
    for j in range(2 * tiles_per_part):
        k0 = ((j % tiles_per_part) // 2) * V7X_LANES
        bu = _dot(u_bf[:, k0:k0 + V7X_LANES], bb[k0:k0 + V7X_LANES, j * MXU_N:(j + 1) * MXU_N])
        for half in range(MXU_N // V7X_LANES):
            hbuf[j * (MXU_N // V7X_LANES) + half] = bu[:, half * V7X_LANES:(half + 1) * V7X_LANES]

    n_sub = nb // V7X_SUBLANES
    for lc in range(N_SLABS // SCAN_SLABS):
        slabs = [lc * SCAN_SLABS + k for k in range(SCAN_SLABS)]
        a_re = [are_ref[:, s * V7X_LANES:(s + 1) * V7X_LANES] for s in slabs]
        a_im = [aim_ref[:, s * V7X_LANES:(s + 1) * V7X_LANES] for s in slabs]

        def sub_body(sg, carry, slabs=slabs, a_re=a_re, a_im=a_im):
            seqs = pl.ds(pl.multiple_of(sg * V7X_SUBLANES, V7X_SUBLANES), V7X_SUBLANES)

            def t_body(t, state):
                rows = step_rows(t, sg)
                new = []
                for k, s in enumerate(slabs):
                    s_re, s_im = state[2 * k], state[2 * k + 1]
                    n_re = a_re[k] * s_re - a_im[k] * s_im + hbuf[s, rows, :]
                    n_im = a_re[k] * s_im + a_im[k] * s_re + hbuf[N_SLABS + s, rows, :]
                    hbuf[s, rows, :] = n_re
                    hbuf[N_SLABS + s, rows, :] = n_im
                    new += [n_re, n_im]
                return tuple(new)

            init = []
            for s in slabs:
                init += [st_ref[seqs, s * V7X_LANES:(s + 1) * V7X_LANES],
                         st_ref[seqs, GP + s * V7X_LANES:GP + (s + 1) * V7X_LANES]]
            fin = lax.fori_loop(0, tt, t_body, tuple(init), unroll=min(tt, SCAN_UNROLL))
            for k, s in enumerate(slabs):
                st_ref[seqs, s * V7X_LANES:(s + 1) * V7X_LANES] = fin[2 * k]
                st_ref[seqs, GP + s * V7X_LANES:GP + (s + 1) * V7X_LANES] = fin[2 * k + 1]
            return carry

        lax.fori_loop(0, n_sub, sub_body, 0)

    y_tiles = []
    k_per_tile = GP * MXU_N // D_SSM
    slabs_per_tile = k_per_tile // V7X_LANES
    for n in range(D_SSM // MXU_N):
        cols = slice(n * MXU_N, (n + 1) * MXU_N)
        acc = None
        for part in range(2):
            s0 = part * N_SLABS + n * slabs_per_tile
            hs = jnp.concatenate([hbuf[s0 + c] for c in range(slabs_per_tile)], axis=1).astype(BF16)
            k0 = part * GP + n * k_per_tile
            term = _dot(hs, cc[k0:k0 + k_per_tile, cols])
            acc = term if acc is None else acc + term
        y_tiles.append(acc)
    y = jnp.concatenate(y_tiles, axis=1)
    if time_major:
        for c in range(n_lane_tiles):
            tbuf[c, 0:r, :] = y[:, c * V7X_LANES:(c + 1) * V7X_LANES]
        y = jnp.concatenate(
            [jnp.concatenate([tbuf[c, pl.ds(b, tt, stride=nb), :] for b in range(nb)], axis=0)
             for c in range(n_lane_tiles)], axis=1)
    y = y + d_skip[...] * u
    y = jax.nn.gelu(y, approximate=True)
    y = y * jax.nn.sigmoid(_dot(y.astype(BF16), w_glu[...]) + b_glu[...])
    ys_ref[...] = _rms(y, ssm_g[...]).astype(BF16).reshape(g, rpg, D_SSM)


def _mixer(x1, cos, sin, h0, w, nb, tt):
    n_g, length, _ = x1.shape
    rpg = nb * tt // n_g
    assert length % rpg == 0 and rpg % 16 == 0 and nb % V7X_SUBLANES == 0
    tbuf_rows = nb * _scan_pitch(tt) if _scan_time_major(nb, tt) else V7X_SUBLANES
    row = lambda width: pl.BlockSpec((n_g, rpg, width), lambda i: (0, i, 0))
    out = lambda width, dtype: jax.ShapeDtypeStruct((n_g, length, width), dtype)
    weight_specs = [
        _const_spec((1, D_MODEL)), _const_spec((D_MODEL, D_IN_PACKED)), _const_spec((D_SSM, 2 * GP)),
        _const_spec((1, GP)), _const_spec((1, GP)), _const_spec((2 * GP, D_SSM)), _const_spec((1, D_SSM)),
        _const_spec((D_SSM, D_SSM)), _const_spec((1, D_SSM)), _const_spec((1, D_SSM)),
        _const_spec((1, Q_LORA)), _const_spec((Q_LORA, D_Q_PACKED)),
        _const_spec((N_HEADS * QK_NOPE, N_HEADS * KV_LORA)), _const_spec((1, KV_LORA)),
    ]
    rope_spec = pl.BlockSpec((rpg, V7X_LANES), lambda i: (i, 0))
    return pl.pallas_call(
        functools.partial(_mixer_kernel, nb, tt),
        grid=(length // rpg,),
        in_specs=[row(D_MODEL), rope_spec, rope_spec, _const_spec((nb, 2 * GP))] + weight_specs,
        out_specs=[row(D_SSM), row(N_HEADS * KV_LORA), row(PE_W), row(KV_LORA + PE_W), row(KV_LORA),
                   row(QK_ROPE), pl.BlockSpec((nb, 2 * GP), lambda i: (0, 0))],
        out_shape=[out(D_SSM, BF16), out(N_HEADS * KV_LORA, BF16), out(PE_W, BF16),
                   out(KV_LORA + PE_W, BF16), out(KV_LORA, F32), out(QK_ROPE, F32),
                   jax.ShapeDtypeStruct((nb, 2 * GP), F32)],
        scratch_shapes=[pltpu.VMEM((2 * N_SLABS, nb * tt, V7X_LANES), F32),
                        pltpu.VMEM((D_SSM // V7X_LANES, tbuf_rows, V7X_LANES), F32)],
        compiler_params=_params(1),
        name="mixer",
    )(x1, cos, sin, h0, *w)


def _uv_project(o_heads, w_uv):
    heads_per_tile = MXU_N // V_DIM
    tiles = []
    for n in range(D_ATT // MXU_N):
        h0 = n * heads_per_tile
        lhs = jnp.concatenate(o_heads[h0:h0 + heads_per_tile], axis=-1)
        tiles.append(_dot(lhs, w_uv[h0 * KV_LORA:(h0 + heads_per_tile) * KV_LORA,
                                    n * MXU_N:(n + 1) * MXU_N]))
    return jnp.concatenate(tiles, axis=-1)


def _prompt_attn_kernel(tq, n_below, qlat_ref, qpe_ref, kcat_ref, w_uv, att_g, ya_ref, acc_ref):
    qpe = qpe_ref[...]
    head_of_lane = lax.shift_right_logical(
        lax.broadcasted_iota(jnp.int32, (tq, PE_W), 1) & (V7X_LANES - 1), 4)
    q_heads = []
    for hd in range(N_HEADS):
        q_heads.append(jnp.concatenate(
            [qlat_ref[:, hd * KV_LORA:(hd + 1) * KV_LORA],
             jnp.where(head_of_lane == hd, qpe, jnp.zeros_like(qpe))], axis=-1))
    hpc = N_HEADS // ATT_ROW_CHUNKS
    rows = hpc * tq
    qs = [jnp.concatenate(q_heads[c * hpc:(c + 1) * hpc], axis=0) for c in range(ATT_ROW_CHUNKS)]

    def blocks(key_blocks, state):
        ks = [kcat_ref[j * tq:(j + 1) * tq, :] for j, _ in key_blocks]
        scores = [[_dot_nt(q, k) for k in ks] for q in qs]
        new_state = []
        for c in range(ATT_ROW_CHUNKS):
            ss = []
            for s, (_, diagonal) in zip(scores[c], key_blocks):
                if diagonal:
                    q_pos = lax.broadcasted_iota(jnp.int32, (rows, tq), 0) & (tq - 1)
                    k_pos = lax.broadcasted_iota(jnp.int32, (rows, tq), 1)
                    s = jnp.where(k_pos <= q_pos, s, -jnp.inf)
                ss.append(s)
            m_new = None
            for s in ss:
                m_s = jnp.max(s, axis=-1, keepdims=True)
                m_new = m_s if m_new is None else jnp.maximum(m_new, m_s)
            if state is not None:
                m, l = state[2 * c], state[2 * c + 1]
                m_new = jnp.maximum(m, m_new)
                alpha = jnp.exp(m - m_new)
            l_new = None
            pv = None
            for s, k in zip(ss, ks):
                p = jnp.exp(s - m_new)
                l_p = jnp.sum(p, axis=-1, keepdims=True)
                l_new = l_p if l_new is None else l_new + l_p
                term = _dot(p.astype(BF16), k[:, :KV_LORA])
                pv = term if pv is None else pv + term
            acc_rows = pl.ds(c * rows, rows)
            if state is None:
                acc_ref[acc_rows, :] = pv
            else:
                l_new = alpha * l + l_new
                acc_ref[acc_rows, :] = alpha * acc_ref[acc_rows, :] + pv
            new_state += [m_new, l_new]
        return tuple(new_state)

    key_blocks = [(j, j == n_below) for j in range(n_below + 1)]
    state = None
    for g0 in range(0, len(key_blocks), ATT_KEY_GROUP):
        state = blocks(key_blocks[g0:g0 + ATT_KEY_GROUP], state)

    l_all = jnp.concatenate([state[2 * c + 1] for c in range(ATT_ROW_CHUNKS)], axis=0)
    o = (acc_ref[...] / l_all).astype(BF16)
    o_heads = [o[hd * tq:(hd + 1) * tq, :] for hd in range(N_HEADS)]
    ya_ref[...] = _rms(_uv_project(o_heads, w_uv), att_g[...]).astype(BF16)


def _prompt_attn(qlat, qpe, kcat, w_uv, att_g):
    bsz, t, _ = qlat.shape
    tq = min(ATT_TQ, t)
    assert t % tq == 0 and tq & (tq - 1) == 0
    outs = []
    for i in range(t // tq):
        outs.append(pl.pallas_call(
            functools.partial(_prompt_attn_kernel, tq, i),
            grid=(bsz,),
            in_specs=[pl.BlockSpec((None, tq, N_HEADS * KV_LORA), lambda b, i=i: (b, i, 0)),
                      pl.BlockSpec((None, tq, PE_W), lambda b, i=i: (b, i, 0)),
                      pl.BlockSpec((None, (i + 1) * tq, KV_LORA + PE_W), lambda b: (b, 0, 0)),
                      _const_spec((N_HEADS * KV_LORA, D_ATT)), _const_spec((1, D_ATT))],
            out_specs=pl.BlockSpec((None, tq, D_ATT), lambda b: (b, 0, 0)),
            out_shape=jax.ShapeDtypeStruct((bsz, tq, D_ATT), BF16),
            scratch_shapes=[pltpu.VMEM((N_HEADS * tq, KV_LORA), F32)],
            compiler_params=_params(1),
            name=f"prompt_attn_q{i}",
        )(qlat, qpe, kcat, w_uv, att_g))
    return jnp.concatenate(outs, axis=1)


def _merge(m, l, acc, m_c, l_c, acc_c):
    m_new = jnp.maximum(m, m_c)
    w = jnp.exp(m - m_new)
    w_c = jnp.exp(m_c - m_new)
    return m_new, w * l + w_c * l_c, w * acc + w_c * acc_c


def _new_token_state(qlat, qpe, cnew_ref, penew_ref, seq, dec_t):
    rows = qlat.shape[0]
    qlat_f = qlat.astype(F32)
    qpe_f = qpe.astype(F32)
    t_of_row = lax.broadcasted_iota(jnp.int32, (rows, 1), 0) & (dec_t - 1)
    s_new = []
    for j in range(dec_t):
        sj = (jnp.sum(qlat_f * cnew_ref[seq, j:j + 1, :], axis=-1, keepdims=True)
              + jnp.sum(qpe_f * penew_ref[seq, j:j + 1, :], axis=-1, keepdims=True))
        s_new.append(jnp.where(t_of_row >= j, sj, -jnp.inf))
    m = s_new[0]
    for j in range(1, dec_t):
        m = jnp.maximum(m, s_new[j])
    l = jnp.zeros((rows, 1), F32)
    acc = jnp.zeros((rows, KV_LORA), F32)
    for j in range(dec_t):
        pj = jnp.exp(s_new[j] - m)
        l = l + pj
        acc = acc + pj * cnew_ref[seq, j:j + 1, :]
    return m, l, acc


def _decode_attn_kernel(n_regions, dec_t,
                        pt_ref, qlat_ref, qpe_ref, cnew_ref, penew_ref, ckv_hbm, kpe_hbm,
                        o_ref, kvbuf, pebuf, sem):
    step = pl.program_id(0)
    n_steps = pl.num_programs(0)
    n_slots = DEC_SEQS_PER_STEP * n_regions

    def page_copies(st, slot, p):
        seq = st * DEC_SEQS_PER_STEP + slot // n_regions
        page = pt_ref[seq, (slot % n_regions) * DEC_REGION_PAGES + p]
        return (pltpu.make_async_copy(ckv_hbm.at[0, page], kvbuf.at[slot, p], sem.at[0, slot]),
                pltpu.make_async_copy(kpe_hbm.at[0, page], pebuf.at[slot, p], sem.at[1, slot]))

    def start_region(st, slot):
        for p in range(DEC_REGION_PAGES):
            for cp in page_copies(st, slot, p):
                cp.start()

    def wait_region(st, slot):
        for p in range(DEC_REGION_PAGES):
            for cp in page_copies(st, slot, p):
                cp.wait()

    @pl.when(step == 0)
    def _():
        for slot in range(DEC_AHEAD):
            start_region(0, slot)

    sub_keys = DEC_SUB_PAGES * V7X_LANES
    for slot in range(n_slots):
        seq = slot // n_regions
        if slot % n_regions == 0:
            qlat = qlat_ref[seq]
            qpe = qpe_ref[seq]
            m, l, acc = _new_token_state(qlat, qpe, cnew_ref, penew_ref, seq, dec_t)
        ahead = slot + DEC_AHEAD
        if ahead < n_slots:
            start_region(step, ahead)
        else:
            @pl.when(step + 1 < n_steps)
            def _(ahead=ahead):
                start_region(step + 1, ahead - n_slots)
        wait_region(step, slot)
        kvs, scores = [], []
        for sub in range(DEC_REGION_PAGES // DEC_SUB_PAGES):
            p0 = sub * DEC_SUB_PAGES
            kv = kvbuf[slot, p0:p0 + DEC_SUB_PAGES].reshape(sub_keys, KV_LORA).astype(BF16)
            pe_t = jnp.concatenate([pebuf[slot, p0 + p] for p in range(DEC_SUB_PAGES)],
                                   axis=1).astype(BF16)
            kvs.append(kv)
            scores.append(_dot_nt(qlat, kv) + _dot(qpe, pe_t))
        for kv, s in zip(kvs, scores):
            m_c = jnp.max(s, axis=-1, keepdims=True)
            p = jnp.exp(s - m_c)
            l_c = jnp.sum(p, axis=-1, keepdims=True)
            acc_c = _dot(p.astype(BF16), kv)
            m, l, acc = _merge(m, l, acc, m_c, l_c, acc_c)
        if slot % n_regions == n_regions - 1:
            o_ref[seq] = acc / l


def _decode_attn(page_table, qlat, qpe, cnew, penew, cache_ckv, cache_kpe_t):
    bsz, n_pages = page_table.shape
    dec_t = cnew.shape[1]
    page = cache_ckv.shape[2]
    assert page == V7X_LANES and n_pages % DEC_REGION_PAGES == 0 and dec_t & (dec_t - 1) == 0
    assert bsz % DEC_SEQS_PER_STEP == 0
    rows = N_HEADS * dec_t
    n_regions = n_pages // DEC_REGION_PAGES
    n_slots = DEC_SEQS_PER_STEP * n_regions
    assert n_slots > DEC_AHEAD, "a slot must not be refilled while it is still being read"
    per_step = lambda r, width: pl.BlockSpec((DEC_SEQS_PER_STEP, r, width), lambda s, pt: (s, 0, 0))
    grid_spec = pltpu.PrefetchScalarGridSpec(
        num_scalar_prefetch=1,
        grid=(bsz // DEC_SEQS_PER_STEP,),
        in_specs=[per_step(rows, KV_LORA), per_step(rows, QK_ROPE), per_step(dec_t, KV_LORA),
                  per_step(dec_t, QK_ROPE),
                  pl.BlockSpec(memory_space=pl.ANY), pl.BlockSpec(memory_space=pl.ANY)],
        out_specs=per_step(rows, KV_LORA),
        scratch_shapes=[pltpu.VMEM((n_slots, DEC_REGION_PAGES, page, KV_LORA), F32),
                        pltpu.VMEM((n_slots, DEC_REGION_PAGES, QK_ROPE, page), F32),
                        pltpu.SemaphoreType.DMA((2, n_slots))],
    )
    return pl.pallas_call(
        functools.partial(_decode_attn_kernel, n_regions, dec_t),
        grid_spec=grid_spec,
        out_shape=jax.ShapeDtypeStruct((bsz, rows, KV_LORA), F32),
        compiler_params=_params(1),
        name="decode_attn",
    )(page_table, qlat, qpe, cnew, penew, cache_ckv, cache_kpe_t)


def _uv_kernel(o_ref, w_uv, att_g, ya_ref):
    o_heads = [o_ref[:, hd * KV_LORA:(hd + 1) * KV_LORA].astype(BF16) for hd in range(N_HEADS)]
    ya_ref[...] = _rms(_uv_project(o_heads, w_uv), att_g[...]).astype(BF16)


def _uv_proj(o_cat, w_uv, att_g):
    r = o_cat.shape[0]
    return pl.pallas_call(
        _uv_kernel,
        out_shape=jax.ShapeDtypeStruct((r, D_ATT), BF16),
        compiler_params=pltpu.CompilerParams(vmem_limit_bytes=VMEM_LIMIT),
        name="uv_proj",
    )(o_cat, w_uv, att_g)


def _rope_tables(pos):
    inv = ROPE_THETA ** (-jnp.arange(0, QK_ROPE, 2, dtype=F32) / QK_ROPE)
    ang = pos[:, None] * inv[None, :]
    return jnp.tile(jnp.cos(ang), (1, N_HEADS)), jnp.tile(jnp.sin(ang), (1, N_HEADS))


def kernel(x_prompt, x_sample, cache_ckv, cache_kpe, state_ssm_re, state_ssm_im, page_table, ffn1_pre_g, ffn1_w_gate, ffn1_w_up, ffn1_w_down, ffn1_post_g, mix_pre_g, w_in, ssm_lam_re, ssm_lam_im, ssm_log_step, ssm_b_re, ssm_b_im, ssm_c_re, ssm_c_im, ssm_d, ssm_w_glu, ssm_b_glu, q_norm_g, w_uq, kv_norm_g, w_uk, w_uv, ssm_out_g, att_out_g, w_out, mix_post_g, ffn2_pre_g, ffn2_w_gate, ffn2_w_up, ffn2_w_down, ffn2_post_g):
    assert ffn1_pre_g.shape[0] == 1, "single-layer trunk"
    bsz, seq, _ = x_prompt.shape
    dbs, dseq, _ = x_sample.shape
    past_len = page_table.shape[1] * cache_ckv.shape[2]
    assert bsz == V7X_SUBLANES

    row = lambda v: v[0].reshape(1, -1)
    ffn1_w = (row(ffn1_pre_g), ffn1_w_gate[0].astype(BF16), ffn1_w_up[0].astype(BF16),
              ffn1_w_down[0].astype(BF16), row(ffn1_post_g))
    ffn2_w = (w_out[0, :D_SSM].astype(BF16), w_out[0, D_SSM:].astype(BF16), row(mix_post_g),
              row(ffn2_pre_g), ffn2_w_gate[0].astype(BF16), ffn2_w_up[0].astype(BF16),
              ffn2_w_down[0].astype(BF16), row(ffn2_post_g))
    off_pe = D_SSM + Q_LORA + KV_LORA
    w_in_p = jnp.concatenate(
        [w_in[0, :, :off_pe],
         jnp.tile(w_in[0, :, off_pe:off_pe + HALF_ROPE], (1, N_HEADS)),
         jnp.tile(w_in[0, :, off_pe + HALF_ROPE:], (1, N_HEADS))], axis=1).astype(BF16)
    wq = w_uq[0]
    w_uq_p = jnp.concatenate(
        [wq[:, :, :QK_NOPE].reshape(Q_LORA, N_HEADS * QK_NOPE),
         wq[:, :, QK_NOPE:QK_NOPE + HALF_ROPE].reshape(Q_LORA, N_HEADS * HALF_ROPE),
         wq[:, :, QK_NOPE + HALF_ROPE:].reshape(Q_LORA, N_HEADS * HALF_ROPE)], axis=1).astype(BF16)
    eye_h = jnp.eye(N_HEADS, dtype=F32)
    w_uk_bd = jnp.einsum('hnr,hg->hngr', w_uk[0].transpose(1, 2, 0), eye_h).reshape(
        N_HEADS * QK_NOPE, N_HEADS * KV_LORA).astype(BF16)
    w_uv_bd = jnp.einsum('hrv,hg->hrgv', w_uv[0].transpose(1, 0, 2), eye_h).reshape(
        N_HEADS * KV_LORA, D_ATT).astype(BF16)
    a_re, a_im, bb, cc = _s5_prep(ssm_lam_re[0], ssm_lam_im[0], ssm_log_step[0],
                                  ssm_b_re[0], ssm_b_im[0], ssm_c_re[0], ssm_c_im[0])
    mix_w = (row(mix_pre_g), w_in_p, bb, a_re, a_im, cc, row(ssm_d), ssm_w_glu[0].astype(BF16),
             row(ssm_b_glu), row(ssm_out_g), row(q_norm_g), w_uq_p, w_uk_bd, row(kv_norm_g))
    att_g = row(att_out_g)

    cos_p, sin_p = _rope_tables(jnp.arange(seq, dtype=F32))
    x1_p = _ffn1(x_prompt.reshape(bsz * seq, D_MODEL), ffn1_w)
    ys_p, qlat_p, qpe_p, kcat_p, ckv_p, kpe_p, st_p = _mixer(
        x1_p.reshape(bsz, seq, D_MODEL), cos_p, sin_p, jnp.zeros((bsz, 2 * GP), F32), mix_w,
        bsz, min(MIX_TT, seq))
    ya_p = _prompt_attn(qlat_p, qpe_p, kcat_p, w_uv_bd, att_g)
    y_p = _ffn2(x1_p, ys_p.reshape(bsz * seq, D_SSM), ya_p.reshape(bsz * seq, D_ATT), ffn2_w)

    n_s = dbs * dseq
    cos_s, sin_s = _rope_tables(past_len + jnp.arange(dseq, dtype=F32))
    x1_s = _ffn1(x_sample.reshape(n_s, D_MODEL), ffn1_w)
    h0_s = jnp.concatenate([state_ssm_re[0].reshape(dbs, GP), state_ssm_im[0].reshape(dbs, GP)], axis=1)
    ys_s, qlat_s, qpe_s, _, ckv_s, kpe_s, st_s = _mixer(
        x1_s.reshape(1, n_s, D_MODEL), jnp.tile(cos_s, (dbs, 1)), jnp.tile(sin_s, (dbs, 1)), h0_s, mix_w,
        dbs, dseq)
    qlat_d = qlat_s.reshape(dbs, dseq, N_HEADS, KV_LORA).transpose(0, 2, 1, 3).reshape(
        dbs, N_HEADS * dseq, KV_LORA)
    qpe_d = qpe_s.reshape(dbs, dseq, 2, N_HEADS, HALF_ROPE).transpose(0, 3, 1, 2, 4).reshape(
        dbs, N_HEADS * dseq, QK_ROPE)
    cnew = ckv_s.reshape(dbs, dseq, KV_LORA)
    penew = kpe_s.reshape(dbs, dseq, QK_ROPE)
    o_d = _decode_attn(page_table, qlat_d, qpe_d, cnew, penew, cache_ckv, cache_kpe.transpose(0, 1, 3, 2))
    o_cat = o_d.reshape(dbs, N_HEADS, dseq, KV_LORA).transpose(0, 2, 1, 3).reshape(n_s, N_HEADS * KV_LORA)
    ya_s = _uv_proj(o_cat, w_uv_bd, att_g)
    y_s = _ffn2(x1_s, ys_s.reshape(n_s, D_SSM), ya_s, ffn2_w)

    state = lambda st, n: (st[:, :GP].reshape(1, n, N_GROUPS, SSM_STATE),
                           st[:, GP:].reshape(1, n, N_GROUPS, SSM_STATE))
    re_p, im_p = state(st_p, bsz)
    re_s, im_s = state(st_s, dbs)
    return (y_p.reshape(bsz, seq, D_MODEL), y_s.reshape(dbs, dseq, D_MODEL),
            ckv_p[None], kpe_p[None], re_p, im_p,
            cnew[None], penew[None], re_s, im_s)
```

```python
import functools
import math

import jax
import jax.numpy as jnp
from jax import lax
from jax.experimental import pallas as pl
from jax.experimental.pallas import tpu as pltpu

F32 = jnp.float32
BF16 = jnp.bfloat16

D_MODEL = 1024
D_FF = 2816
EPS = 1e-6
D_SSM = 512
SSM_GROUP = 16
N_GROUPS = D_SSM // SSM_GROUP
SSM_STATE = 64
GP = N_GROUPS * SSM_STATE
N_HEADS = 8
QK_NOPE = 64
QK_ROPE = 32
HALF_ROPE = QK_ROPE // 2
V_DIM = 64
D_ATT = N_HEADS * V_DIM
Q_LORA = 256
KV_LORA = 256
ROPE_THETA = 10000.0
SOFTMAX_SCALE = 1.0 / math.sqrt(QK_NOPE + QK_ROPE)
D_IN_PACKED = D_SSM + Q_LORA + KV_LORA + 2 * N_HEADS * HALF_ROPE
D_Q_PACKED = N_HEADS * QK_NOPE + 2 * N_HEADS * HALF_ROPE
PE_W = 2 * N_HEADS * HALF_ROPE

V7X_SUBLANES = 8
V7X_LANES = 128
MXU_N = 256
VMEM_LIMIT = 56 * 1024 * 1024

FFN_TM = 512
FFN_SUB_TILES = 2
MIX_TT = 64
ATT_TQ = 256
ATT_ROW_CHUNKS = 8
ATT_KEY_GROUP = 8
DEC_REGION_PAGES = 64
DEC_SEQS_PER_STEP = 2
DEC_AHEAD = 3
DEC_SUB_PAGES = 16
SCAN_SLABS = 4
SCAN_UNROLL = 64
N_SLABS = GP // V7X_LANES


def _const_spec(shape):
    nd = len(shape)
    return pl.BlockSpec(shape, lambda *_: (0,) * nd, pipeline_mode=pl.Buffered(1))


def _rms(x, g):
    return x * lax.rsqrt(jnp.mean(x * x, axis=-1, keepdims=True) + EPS) * g


def _dot(a, b):
    return jnp.dot(a, b, preferred_element_type=F32)


def _dot_nt(a, b):
    return lax.dot_general(a, b, (((1,), (1,)), ((), ())), preferred_element_type=F32)


def _ffn(xs, pre_g, wg_ref, wu_ref, wd_ref, post_g):
    hs = [_rms(x, pre_g).astype(BF16) for x in xs]
    gates = [_dot(h, wg_ref[...]) for h in hs]
    ups = [_dot(h, wu_ref[...]) for h in hs]
    acts = [(gate * jax.nn.sigmoid(gate) * up).astype(BF16) for gate, up in zip(gates, ups)]
    fs = [_dot(act, wd_ref[...]) for act in acts]
    return [x + 0.5 * _rms(f, post_g) for x, f in zip(xs, fs)]


def _params(n_grid_axes):
    return pltpu.CompilerParams(dimension_semantics=("arbitrary",) * n_grid_axes,
                                vmem_limit_bytes=VMEM_LIMIT)


def _sub_tiles(n_rows):
    sub = n_rows // FFN_SUB_TILES if n_rows % (FFN_SUB_TILES * 16) == 0 else n_rows
    return [pl.ds(r0, sub) for r0 in range(0, n_rows, sub)]


def _ffn1_kernel(x_ref, pre_g, wg, wu, wd, post_g, o_ref):
    tiles = _sub_tiles(x_ref.shape[0])
    outs = _ffn([x_ref[rows, :] for rows in tiles], pre_g[...], wg, wu, wd, post_g[...])
    for rows, out in zip(tiles, outs):
        o_ref[rows, :] = out


def _ffn_weight_specs():
    return [_const_spec((1, D_MODEL)), _const_spec((D_MODEL, D_FF)), _const_spec((D_MODEL, D_FF)),
            _const_spec((D_FF, D_MODEL)), _const_spec((1, D_MODEL))]


def _ffn1(x, w):
    r = x.shape[0]
    tm = min(FFN_TM, r)
    rows = pl.BlockSpec((tm, D_MODEL), lambda i: (i, 0))
    return pl.pallas_call(
        _ffn1_kernel,
        grid=(r // tm,),
        in_specs=[rows] + _ffn_weight_specs(),
        out_specs=rows,
        out_shape=jax.ShapeDtypeStruct((r, D_MODEL), F32),
        compiler_params=_params(1),
        name="ffn1",
    )(x, *w)


def _ffn2_kernel(x_ref, ys_ref, ya_ref, wo_s, wo_a, mix_g, pre_g, wg, wu, wd, post_g, o_ref):
    tiles = _sub_tiles(x_ref.shape[0])
    xs = []
    for rows in tiles:
        mixed = _dot(ys_ref[rows, :], wo_s[...]) + _dot(ya_ref[rows, :], wo_a[...])
        xs.append(x_ref[rows, :] + _rms(mixed, mix_g[...]))
    outs = _ffn(xs, pre_g[...], wg, wu, wd, post_g[...])
    for rows, out in zip(tiles, outs):
        o_ref[rows, :] = out


def _ffn2(x1, ys, ya, w):
    r = x1.shape[0]
    tm = min(FFN_TM, r)
    rows = lambda width: pl.BlockSpec((tm, width), lambda i: (i, 0))
    weight_specs = [_const_spec((D_SSM, D_MODEL)), _const_spec((D_ATT, D_MODEL)),
                    _const_spec((1, D_MODEL))] + _ffn_weight_specs()
    return pl.pallas_call(
        _ffn2_kernel,
        grid=(r // tm,),
        in_specs=[rows(D_MODEL), rows(D_SSM), rows(D_ATT)] + weight_specs,
        out_specs=rows(D_MODEL),
        out_shape=jax.ShapeDtypeStruct((r, D_MODEL), F32),
        compiler_params=_params(1),
        name="ffn2",
    )(x1, ys, ya, *w)


def _s5_prep_kernel(lr_ref, li_ref, ls_ref, br_ref, bi_ref, cr_ref, ci_ref,
                    are_ref, aim_ref, bb_ref, cc_ref):
    lr = lr_ref[...]
    li = li_ref[...]
    dt = jnp.exp(ls_ref[...])
    mag = jnp.exp(lr * dt)
    a_re = mag * jnp.cos(li * dt)
    a_im = mag * jnp.sin(li * dt)
    den = lr * lr + li * li
    n_re = a_re - 1.0
    z_re = (n_re * lr + a_im * li) / den
    z_im = (a_im * lr - n_re * li) / den
    are_ref[...] = a_re
    aim_ref[...] = a_im

    row_g = lax.shift_right_logical(lax.broadcasted_iota(jnp.int32, (D_SSM, GP), 0), 4)
    col_g = lax.shift_right_logical(lax.broadcasted_iota(jnp.int32, (D_SSM, GP), 1), 6)
    same = row_g == col_g
    br = br_ref[...]
    bi = bi_ref[...]
    bb_ref[:, :GP] = jnp.where(same, z_re * br - z_im * bi, 0.0).astype(BF16)
    bb_ref[:, GP:] = jnp.where(same, z_re * bi + z_im * br, 0.0).astype(BF16)

    row_g = lax.shift_right_logical(lax.broadcasted_iota(jnp.int32, (GP, D_SSM), 0), 6)
    col_g = lax.shift_right_logical(lax.broadcasted_iota(jnp.int32, (GP, D_SSM), 1), 4)
    same = row_g == col_g
    cc_ref[:GP, :] = jnp.where(same, cr_ref[...], 0.0).astype(BF16)
    cc_ref[GP:, :] = jnp.where(same, -ci_ref[...], 0.0).astype(BF16)


def _s5_prep(lam_re, lam_im, log_step, b_re, b_im, c_re, c_im):
    lr = lam_re.reshape(1, GP)
    li = lam_im.reshape(1, GP)
    ls = jnp.repeat(log_step, SSM_STATE).reshape(1, GP)
    br = jnp.tile(b_re.transpose(2, 0, 1).reshape(SSM_GROUP, GP), (N_GROUPS, 1))
    bi = jnp.tile(b_im.transpose(2, 0, 1).reshape(SSM_GROUP, GP), (N_GROUPS, 1))
    cr = jnp.tile(c_re.transpose(0, 2, 1).reshape(GP, SSM_GROUP), (1, N_GROUPS))
    ci = jnp.tile(c_im.transpose(0, 2, 1).reshape(GP, SSM_GROUP), (1, N_GROUPS))
    return pl.pallas_call(
        _s5_prep_kernel,
        out_shape=(jax.ShapeDtypeStruct((1, GP), F32), jax.ShapeDtypeStruct((1, GP), F32),
                   jax.ShapeDtypeStruct((D_SSM, 2 * GP), BF16), jax.ShapeDtypeStruct((2 * GP, D_SSM), BF16)),
        compiler_params=pltpu.CompilerParams(vmem_limit_bytes=VMEM_LIMIT),
        name="s5_prep",
    )(lr, li, ls, br, bi, cr, ci)


def _scan_time_major(nb, tt):
    return tt % V7X_SUBLANES == 0 and nb % V7X_SUBLANES == 0


def _scan_pitch(tt):
    return tt if tt % 16 else tt + V7X_SUBLANES


def _mixer_kernel(nb, tt,
                  x_ref, cos_ref, sin_ref, h0_ref, pre_g, w_in, bb, are_ref, aim_ref, cc, d_skip,
                  w_glu, b_glu, ssm_g, q_g, w_uq, w_uk, kv_g,
                  ys_ref, qlat_ref, qpe_ref, kcat_ref, ckv_ref, kpe_ref, st_ref,
                  hbuf, tbuf):
    g, rpg, _ = x_ref.shape
    r = nb * tt
    time_major = _scan_time_major(nb, tt)
    pitch = _scan_pitch(tt)

    @pl.when(pl.program_id(0) == 0)
    def _():
        st_ref[...] = h0_ref[...]

    x = x_ref[...].reshape(r, D_MODEL)
    h = _rms(x, pre_g[...]).astype(BF16)
    z = _dot(h, w_in[...])
    u = z[:, :D_SSM]
    hq = z[:, D_SSM:D_SSM + Q_LORA]
    hkv = z[:, D_SSM + Q_LORA:D_SSM + Q_LORA + KV_LORA]
    hpe = z[:, D_SSM + Q_LORA + KV_LORA:]

    cos = cos_ref[...]
    sin = sin_ref[...]

    def rope(x1, x2):
        x1 = x1.reshape(g, rpg, V7X_LANES)
        x2 = x2.reshape(g, rpg, V7X_LANES)
        return x1 * cos - x2 * sin, x2 * cos + x1 * sin

    q = _dot(_rms(hq, q_g[...]).astype(BF16), w_uq[...]) * SOFTMAX_SCALE
    q_nope = q[:, :N_HEADS * QK_NOPE]
    q1, q2 = rope(q[:, N_HEADS * QK_NOPE:N_HEADS * QK_NOPE + V7X_LANES],
                  q[:, N_HEADS * QK_NOPE + V7X_LANES:])
    qpe_ref[:, :, :V7X_LANES] = q1.astype(BF16)
    qpe_ref[:, :, V7X_LANES:] = q2.astype(BF16)
    q_nope = q_nope.astype(BF16)
    for hd in range(N_HEADS):
        k0 = (hd * QK_NOPE // V7X_LANES) * V7X_LANES
        cols = slice(hd * KV_LORA, (hd + 1) * KV_LORA)
        qlat_ref[:, :, cols] = _dot(q_nope[:, k0:k0 + V7X_LANES], w_uk[k0:k0 + V7X_LANES, cols]).astype(
            BF16).reshape(g, rpg, KV_LORA)

    ckv = _rms(hkv, kv_g[...]).reshape(g, rpg, KV_LORA)
    ckv_ref[...] = ckv
    k1, k2 = rope(hpe[:, :V7X_LANES], hpe[:, V7X_LANES:])
    kcat_ref[:, :, :KV_LORA] = ckv.astype(BF16)
    kcat_ref[:, :, KV_LORA:KV_LORA + V7X_LANES] = k1.astype(BF16)
    kcat_ref[:, :, KV_LORA + V7X_LANES:] = k2.astype(BF16)
    kpe_ref[:, :, :HALF_ROPE] = k1[:, :, :HALF_ROPE]
    kpe_ref[:, :, HALF_ROPE:] = k2[:, :, :HALF_ROPE]

    n_lane_tiles = D_SSM // V7X_LANES
    if time_major:
        for c in range(n_lane_tiles):
            for b in range(nb):
                tbuf[c, b * pitch:b * pitch + tt, :] = u[b * tt:(b + 1) * tt, c * V7X_LANES:(c + 1) * V7X_LANES]
        u_rows = jnp.concatenate(
            [jnp.concatenate([tbuf[c, pl.ds(t, nb, stride=pitch), :] for t in range(tt)], axis=0)
             for c in range(n_lane_tiles)], axis=1)

        def step_rows(t, sg):
            return pl.ds(pl.multiple_of(t * nb + sg * V7X_SUBLANES, V7X_SUBLANES), V7X_SUBLANES)
    else:
        u_rows = u

        def step_rows(t, sg):
            return pl.ds(sg * (V7X_SUBLANES * tt) + t, V7X_SUBLANES, stride=tt)

    u_bf = u_rows.astype(BF16)
    tiles_per_part = GP // MXU_N
    for j in range(2 * tiles_per_part):
        k0 = ((j % tiles_per_part) // 2) * V7X_LANES
        bu = _dot(u_bf[:, k0:k0 + V7X_LANES], bb[k0:k0 + V7X_LANES, j * MXU_N:(j + 1) * MXU_N])
        for half in range(MXU_N // V7X_LANES):
            hbuf[j * (MXU_N // V7X_LANES) + half] = bu[:, half * V7X_LANES:(half + 1) * V7X_LANES]

    n_sub = nb // V7X_SUBLANES
    for lc in range(N_SLABS // SCAN_SLABS):
        slabs = [lc * SCAN_SLABS + k for k in range(SCAN_SLABS)]
        a_re = [are_ref[:, s * V7X_LANES:(s + 1) * V7X_LANES] for s in slabs]
        a_im = [aim_ref[:, s * V7X_LANES:(s + 1) * V7X_LANES] for s in slabs]

        def sub_body(sg, carry, slabs=slabs, a_re=a_re, a_im=a_im):
            seqs = pl.ds(pl.multiple_of(sg * V7X_SUBLANES, V7X_SUBLANES), V7X_SUBLANES)

            def t_body(t, state):
                rows = step_rows(t, sg)
                new = []
                for k, s in enumerate(slabs):
                    s_re, s_im = state[2 * k], state[2 * k + 1]
                    n_re = a_re[k] * s_re - a_im[k] * s_im + hbuf[s, rows, :]
                    n_im = a_re[k] * s_im + a_im[k] * s_re + hbuf[N_SLABS + s, rows, :]
                    hbuf[s, rows, :] = n_re
                    hbuf[N_SLABS + s, rows, :] = n_im
                    new += [n_re, n_im]
                return tuple(new)

            init = []
            for s in slabs:
                init += [st_ref[seqs, s * V7X_LANES:(s + 1) * V7X_LANES],
                         st_ref[seqs, GP + s * V7X_LANES:GP + (s + 1) * V7X_LANES]]
            fin = lax.fori_loop(0, tt, t_body, tuple(init), unroll=min(tt, SCAN_UNROLL))
            for k, s in enumerate(slabs):
                st_ref[seqs, s * V7X_LANES:(s + 1) * V7X_LANES] = fin[2 * k]
                st_ref[seqs, GP + s * V7X_LANES:GP + (s + 1) * V7X_LANES] = fin[2 * k + 1]
            return carry

        lax.fori_loop(0, n_sub, sub_body, 0)

    y_tiles = []
    k_per_tile = GP * MXU_N // D_SSM
    slabs_per_tile = k_per_tile // V7X_LANES
    for n in range(D_SSM // MXU_N):
        cols = slice(n * MXU_N, (n + 1) * MXU_N)
        acc = None
        for part in range(2):
            s0 = part * N_SLABS + n * slabs_per_tile
            hs = jnp.concatenate([hbuf[s0 + c] for c in range(slabs_per_tile)], axis=1).astype(BF16)
            k0 = part * GP + n * k_per_tile
            term = _dot(hs, cc[k0:k0 + k_per_tile, cols])
            acc = term if acc is None else acc + term
        y_tiles.append(acc)
    y = jnp.concatenate(y_tiles, axis=1)
    if time_major:
        for c in range(n_lane_tiles):
            tbuf[c, 0:r, :] = y[:, c * V7X_LANES:(c + 1) * V7X_LANES]
        y = jnp.concatenate(
            [jnp.concatenate([tbuf[c, pl.ds(b, tt, stride=nb), :] for b in range(nb)], axis=0)
             for c in range(n_lane_tiles)], axis=1)
    y = y + d_skip[...] * u
    y = jax.nn.gelu(y, approximate=True)
    y = y * jax.nn.sigmoid(_dot(y.astype(BF16), w_glu[...]) + b_glu[...])
    ys_ref[...] = _rms(y, ssm_g[...]).astype(BF16).reshape(g, rpg, D_SSM)


def _mixer(x1, cos, sin, h0, w, nb, tt):
    n_g, length, _ = x1.shape
    rpg = nb * tt // n_g
    assert length % rpg == 0 and rpg % 16 == 0 and nb % V7X_SUBLANES == 0
    tbuf_rows = nb * _scan_pitch(tt) if _scan_time_major(nb, tt) else V7X_SUBLANES
    row = lambda width: pl.BlockSpec((n_g, rpg, width), lambda i: (0, i, 0))
    out = lambda width, dtype: jax.ShapeDtypeStruct((n_g, length, width), dtype)
    weight_specs = [
        _const_spec((1, D_MODEL)), _const_spec((D_MODEL, D_IN_PACKED)), _const_spec((D_SSM, 2 * GP)),
        _const_spec((1, GP)), _const_spec((1, GP)), _const_spec((2 * GP, D_SSM)), _const_spec((1, D_SSM)),
        _const_spec((D_SSM, D_SSM)), _const_spec((1, D_SSM)), _const_spec((1, D_SSM)),
        _const_spec((1, Q_LORA)), _const_spec((Q_LORA, D_Q_PACKED)),
        _const_spec((N_HEADS * QK_NOPE, N_HEADS * KV_LORA)), _const_spec((1, KV_LORA)),
    ]
    rope_spec = pl.BlockSpec((rpg, V7X_LANES), lambda i: (i, 0))
    return pl.pallas_call(
        functools.partial(_mixer_kernel, nb, tt),
        grid=(length // rpg,),
        in_specs=[row(D_MODEL), rope_spec, rope_spec, _const_spec((nb, 2 * GP))] + weight_specs,
        out_specs=[row(D_SSM), row(N_HEADS * KV_LORA), row(PE_W), row(KV_LORA + PE_W), row(KV_LORA),
                   row(QK_ROPE), pl.BlockSpec((nb, 2 * GP), lambda i: (0, 0))],
        out_shape=[out(D_SSM, BF16), out(N_HEADS * KV_LORA, BF16), out(PE_W, BF16),
                   out(KV_LORA + PE_W, BF16), out(KV_LORA, F32), out(QK_ROPE, F32),
                   jax.ShapeDtypeStruct((nb, 2 * GP), F32)],
        scratch_shapes=[pltpu.VMEM((2 * N_SLABS, nb * tt, V7X_LANES), F32),
                        pltpu.VMEM((D_SSM // V7X_LANES, tbuf_rows, V7X_LANES), F32)],
        compiler_params=_params(1),
        name="mixer",
    )(x1, cos, sin, h0, *w)


def _uv_project(o_heads, w_uv):
    heads_per_tile = MXU_N // V_DIM
    tiles = []
    for n in range(D_ATT // MXU_N):
        h0 = n * heads_per_tile
        lhs = jnp.concatenate(o_heads[h0:h0 + heads_per_tile], axis=-1)
        tiles.append(_dot(lhs, w_uv[h0 * KV_LORA:(h0 + heads_per_tile) * KV_LORA,
                                    n * MXU_N:(n + 1) * MXU_N]))
    return jnp.concatenate(tiles, axis=-1)


def _prompt_attn_kernel(tq, n_below, qlat_ref, qpe_ref, kcat_ref, w_uv, att_g, ya_ref, acc_ref):
    qpe = qpe_ref[...]
    head_of_lane = lax.shift_right_logical(
        lax.broadcasted_iota(jnp.int32, (tq, PE_W), 1) & (V7X_LANES - 1), 4)
    q_heads = []
    for hd in range(N_HEADS):
        q_heads.append(jnp.concatenate(
            [qlat_ref[:, hd * KV_LORA:(hd + 1) * KV_LORA],
             jnp.where(head_of_lane == hd, qpe, jnp.zeros_like(qpe))], axis=-1))
    hpc = N_HEADS // ATT_ROW_CHUNKS
    rows = hpc * tq
    qs = [jnp.concatenate(q_heads[c * hpc:(c + 1) * hpc], axis=0) for c in range(ATT_ROW_CHUNKS)]

    def blocks(key_blocks, state):
        ks = [kcat_ref[j * tq:(j + 1) * tq, :] for j, _ in key_blocks]
        scores = [[_dot_nt(q, k) for k in ks] for q in qs]
        new_state = []
        for c in range(ATT_ROW_CHUNKS):
            ss = []
            for s, (_, diagonal) in zip(scores[c], key_blocks):
                if diagonal:
                    q_pos = lax.broadcasted_iota(jnp.int32, (rows, tq), 0) & (tq - 1)
                    k_pos = lax.broadcasted_iota(jnp.int32, (rows, tq), 1)
                    s = jnp.where(k_pos <= q_pos, s, -jnp.inf)
                ss.append(s)
            m_new = None
            for s in ss:
                m_s = jnp.max(s, axis=-1, keepdims=True)
                m_new = m_s if m_new is None else jnp.maximum(m_new, m_s)
            if state is not None:
                m, l = state[2 * c], state[2 * c + 1]
                m_new = jnp.maximum(m, m_new)
                alpha = jnp.exp(m - m_new)
            l_new = None
            pv = None
            for s, k in zip(ss, ks):
                p = jnp.exp(s - m_new)
                l_p = jnp.sum(p, axis=-1, keepdims=True)
                l_new = l_p if l_new is None else l_new + l_p
                term = _dot(p.astype(BF16), k[:, :KV_LORA])
                pv = term if pv is None else pv + term
            acc_rows = pl.ds(c * rows, rows)
            if state is None:
                acc_ref[acc_rows, :] = pv
            else:
                l_new = alpha * l + l_new
                acc_ref[acc_rows, :] = alpha * acc_ref[acc_rows, :] + pv
            new_state += [m_new, l_new]
        return tuple(new_state)

    key_blocks = [(j, j == n_below) for j in range(n_below + 1)]
    state = None
    for g0 in range(0, len(key_blocks), ATT_KEY_GROUP):
        state = blocks(key_blocks[g0:g0 + ATT_KEY_GROUP], state)

    l_all = jnp.concatenate([state[2 * c + 1] for c in range(ATT_ROW_CHUNKS)], axis=0)
    o = (acc_ref[...] / l_all).astype(BF16)
    o_heads = [o[hd * tq:(hd + 1) * tq, :] for hd in range(N_HEADS)]
    ya_ref[...] = _rms(_uv_project(o_heads, w_uv), att_g[...]).astype(BF16)


def _prompt_attn(qlat, qpe, kcat, w_uv, att_g):
    bsz, t, _ = qlat.shape
    tq = min(ATT_TQ, t)
    assert t % tq == 0 and tq & (tq - 1) == 0
    outs = []
    for i in range(t // tq):
        outs.append(pl.pallas_call(
            functools.partial(_prompt_attn_kernel, tq, i),
            grid=(bsz,),
            in_specs=[pl.BlockSpec((None, tq, N_HEADS * KV_LORA), lambda b, i=i: (b, i, 0)),
                      pl.BlockSpec((None, tq, PE_W), lambda b, i=i: (b, i, 0)),
                      pl.BlockSpec((None, (i + 1) * tq, KV_LORA + PE_W), lambda b: (b, 0, 0)),
                      _const_spec((N_HEADS * KV_LORA, D_ATT)), _const_spec((1, D_ATT))],
            out_specs=pl.BlockSpec((None, tq, D_ATT), lambda b: (b, 0, 0)),
            out_shape=jax.ShapeDtypeStruct((bsz, tq, D_ATT), BF16),
            scratch_shapes=[pltpu.VMEM((N_HEADS * tq, KV_LORA), F32)],
            compiler_params=_params(1),
            name=f"prompt_attn_q{i}",
        )(qlat, qpe, kcat, w_uv, att_g))
    return jnp.concatenate(outs, axis=1)


def _merge(m, l, acc, m_c, l_c, acc_c):
    m_new = jnp.maximum(m, m_c)
    w = jnp.exp(m - m_new)
    w_c = jnp.exp(m_c - m_new)
    return m_new, w * l + w_c * l_c, w * acc + w_c * acc_c


def _new_token_state(qlat, qpe, cnew_ref, penew_ref, seq, dec_t):
    rows = qlat.shape[0]
    qlat_f = qlat.astype(F32)
    qpe_f = qpe.astype(F32)
    t_of_row = lax.broadcasted_iota(jnp.int32, (rows, 1), 0) & (dec_t - 1)
    s_new = []
    for j in range(dec_t):
        sj = (jnp.sum(qlat_f * cnew_ref[seq, j:j + 1, :], axis=-1, keepdims=True)
              + jnp.sum(qpe_f * penew_ref[seq, j:j + 1, :], axis=-1, keepdims=True))
        s_new.append(jnp.where(t_of_row >= j, sj, -jnp.inf))
    m = s_new[0]
    for j in range(1, dec_t):
        m = jnp.maximum(m, s_new[j])
    l = jnp.zeros((rows, 1), F32)
    acc = jnp.zeros((rows, KV_LORA), F32)
    for j in range(dec_t):
        pj = jnp.exp(s_new[j] - m)
        l = l + pj
        acc = acc + pj * cnew_ref[seq, j:j + 1, :]
    return m, l, acc


def _decode_attn_kernel(n_regions, dec_t,
                        pt_ref, qlat_ref, qpe_ref, cnew_ref, penew_ref, ckv_hbm, kpe_hbm,
                        o_ref, kvbuf, pebuf, sem):
    step = pl.program_id(0)
    n_steps = pl.num_programs(0)
    n_slots = DEC_SEQS_PER_STEP * n_regions

    def page_copies(st, slot, p):
        seq = st * DEC_SEQS_PER_STEP + slot // n_regions
        page = pt_ref[seq, (slot % n_regions) * DEC_REGION_PAGES + p]
        return (pltpu.make_async_copy(ckv_hbm.at[0, page], kvbuf.at[slot, p], sem.at[0, slot]),
                pltpu.make_async_copy(kpe_hbm.at[0, page], pebuf.at[slot, p], sem.at[1, slot]))

    def start_region(st, slot):
        for p in range(DEC_REGION_PAGES):
            for cp in page_copies(st, slot, p):
                cp.start()

    def wait_region(st, slot):
        for p in range(DEC_REGION_PAGES):
            for cp in page_copies(st, slot, p):
                cp.wait()

    @pl.when(step == 0)
    def _():
        for slot in range(DEC_AHEAD):
            start_region(0, slot)

    sub_keys = DEC_SUB_PAGES * V7X_LANES
    for slot in range(n_slots):
        seq = slot // n_regions
        if slot % n_regions == 0:
            qlat = qlat_ref[seq]
            qpe = qpe_ref[seq]
            m, l, acc = _new_token_state(qlat, qpe, cnew_ref, penew_ref, seq, dec_t)
        ahead = slot + DEC_AHEAD
        if ahead < n_slots:
            start_region(step, ahead)
        else:
            @pl.when(step + 1 < n_steps)
            def _(ahead=ahead):
                start_region(step + 1, ahead - n_slots)
        wait_region(step, slot)
        kvs, scores = [], []
        for sub in range(DEC_REGION_PAGES // DEC_SUB_PAGES):
            p0 = sub * DEC_SUB_PAGES
            kv = kvbuf[slot, p0:p0 + DEC_SUB_PAGES].reshape(sub_keys, KV_LORA).astype(BF16)
            pe_t = jnp.concatenate([pebuf[slot, p0 + p] for p in range(DEC_SUB_PAGES)],
                                   axis=1).astype(BF16)
            kvs.append(kv)
            scores.append(_dot_nt(qlat, kv) + _dot(qpe, pe_t))
        for kv, s in zip(kvs, scores):
            m_c = jnp.max(s, axis=-1, keepdims=True)
            p = jnp.exp(s - m_c)
            l_c = jnp.sum(p, axis=-1, keepdims=True)
            acc_c = _dot(p.astype(BF16), kv)
            m, l, acc = _merge(m, l, acc, m_c, l_c, acc_c)
        if slot % n_regions == n_regions - 1:
            o_ref[seq] = acc / l


def _decode_attn(page_table, qlat, qpe, cnew, penew, cache_ckv, cache_kpe_t):
    bsz, n_pages = page_table.shape
    dec_t = cnew.shape[1]
    page = cache_ckv.shape[2]
    assert page == V7X_LANES and n_pages % DEC_REGION_PAGES == 0 and dec_t & (dec_t - 1) == 0
    assert bsz % DEC_SEQS_PER_STEP == 0
    rows = N_HEADS * dec_t
    n_regions = n_pages // DEC_REGION_PAGES
    n_slots = DEC_SEQS_PER_STEP * n_regions
    assert n_slots > DEC_AHEAD, "a slot must not be refilled while it is still being read"
    per_step = lambda r, width: pl.BlockSpec((DEC_SEQS_PER_STEP, r, width), lambda s, pt: (s, 0, 0))
    grid_spec = pltpu.PrefetchScalarGridSpec(
        num_scalar_prefetch=1,
        grid=(bsz // DEC_SEQS_PER_STEP,),
        in_specs=[per_step(rows, KV_LORA), per_step(rows, QK_ROPE), per_step(dec_t, KV_LORA),
                  per_step(dec_t, QK_ROPE),
                  pl.BlockSpec(memory_space=pl.ANY), pl.BlockSpec(memory_space=pl.ANY)],
        out_specs=per_step(rows, KV_LORA),
        scratch_shapes=[pltpu.VMEM((n_slots, DEC_REGION_PAGES, page, KV_LORA), F32),
                        pltpu.VMEM((n_slots, DEC_REGION_PAGES, QK_ROPE, page), F32),
                        pltpu.SemaphoreType.DMA((2, n_slots))],
    )
    return pl.pallas_call(
        functools.partial(_decode_attn_kernel, n_regions, dec_t),
        grid_spec=grid_spec,
        out_shape=jax.ShapeDtypeStruct((bsz, rows, KV_LORA), F32),
        compiler_params=_params(1),
        name="decode_attn",
    )(page_table, qlat, qpe, cnew, penew, cache_ckv, cache_kpe_t)


def _uv_kernel(o_ref, w_uv, att_g, ya_ref):
    o_heads = [o_ref[:, hd * KV_LORA:(hd + 1) * KV_LORA].astype(BF16) for hd in range(N_HEADS)]
    ya_ref[...] = _rms(_uv_project(o_heads, w_uv), att_g[...]).astype(BF16)


def _uv_proj(o_cat, w_uv, att_g):
    r = o_cat.shape[0]
    return pl.pallas_call(
        _uv_kernel,
        out_shape=jax.ShapeDtypeStruct((r, D_ATT), BF16),
        compiler_params=pltpu.CompilerParams(vmem_limit_bytes=VMEM_LIMIT),
        name="uv_proj",
    )(o_cat, w_uv, att_g)


def _rope_tables(pos):
    inv = ROPE_THETA ** (-jnp.arange(0, QK_ROPE, 2, dtype=F32) / QK_ROPE)
    ang = pos[:, None] * inv[None, :]
    return jnp.tile(jnp.cos(ang), (1, N_HEADS)), jnp.tile(jnp.sin(ang), (1, N_HEADS))


def kernel(x_prompt, x_sample, cache_ckv, cache_kpe, state_ssm_re, state_ssm_im, page_table, ffn1_pre_g, ffn1_w_gate, ffn1_w_up, ffn1_w_down, ffn1_post_g, mix_pre_g, w_in, ssm_lam_re, ssm_lam_im, ssm_log_step, ssm_b_re, ssm_b_im, ssm_c_re, ssm_c_im, ssm_d, ssm_w_glu, ssm_b_glu, q_norm_g, w_uq, kv_norm_g, w_uk, w_uv, ssm_out_g, att_out_g, w_out, mix_post_g, ffn2_pre_g, ffn2_w_gate, ffn2_w_up, ffn2_w_down, ffn2_post_g):
    assert ffn1_pre_g.shape[0] == 1, "single-layer trunk"
    bsz, seq, _ = x_prompt.shape
    dbs, dseq, _ = x_sample.shape
    past_len = page_table.shape[1] * cache_ckv.shape[2]
    assert bsz == V7X_SUBLANES

    row = lambda v: v[0].reshape(1, -1)
    ffn1_w = (row(ffn1_pre_g), ffn1_w_gate[0].astype(BF16), ffn1_w_up[0].astype(BF16),
              ffn1_w_down[0].astype(BF16), row(ffn1_post_g))
    ffn2_w = (w_out[0, :D_SSM].astype(BF16), w_out[0, D_SSM:].astype(BF16), row(mix_post_g),
              row(ffn2_pre_g), ffn2_w_gate[0].astype(BF16), ffn2_w_up[0].astype(BF16),
              ffn2_w_down[0].astype(BF16), row(ffn2_post_g))
    off_pe = D_SSM + Q_LORA + KV_LORA
    w_in_p = jnp.concatenate(
        [w_in[0, :, :off_pe],
         jnp.tile(w_in[0, :, off_pe:off_pe + HALF_ROPE], (1, N_HEADS)),
         jnp.tile(w_in[0, :, off_pe + HALF_ROPE:], (1, N_HEADS))], axis=1).astype(BF16)
    wq = w_uq[0]
    w_uq_p = jnp.concatenate(
        [wq[:, :, :QK_NOPE].reshape(Q_LORA, N_HEADS * QK_NOPE),
         wq[:, :, QK_NOPE:QK_NOPE + HALF_ROPE].reshape(Q_LORA, N_HEADS * HALF_ROPE),
         wq[:, :, QK_NOPE + HALF_ROPE:].reshape(Q_LORA, N_HEADS * HALF_ROPE)], axis=1).astype(BF16)
    eye_h = jnp.eye(N_HEADS, dtype=F32)
    w_uk_bd = jnp.einsum('hnr,hg->hngr', w_uk[0].transpose(1, 2, 0), eye_h).reshape(
        N_HEADS * QK_NOPE, N_HEADS * KV_LORA).astype(BF16)
    w_uv_bd = jnp.einsum('hrv,hg->hrgv', w_uv[0].transpose(1, 0, 2), eye_h).reshape(
        N_HEADS * KV_LORA, D_ATT).astype(BF16)
    a_re, a_im, bb, cc = _s5_prep(ssm_lam_re[0], ssm_lam_im[0], ssm_log_step[0],
                                  ssm_b_re[0], ssm_b_im[0], ssm_c_re[0], ssm_c_im[0])
    mix_w = (row(mix_pre_g), w_in_p, bb, a_re, a_im, cc, row(ssm_d), ssm_w_glu[0].astype(BF16),
             row(ssm_b_glu), row(ssm_out_g), row(q_norm_g), w_uq_p, w_uk_bd, row(kv_norm_g))
    att_g = row(att_out_g)

    cos_p, sin_p = _rope_tables(jnp.arange(seq, dtype=F32))
    x1_p = _ffn1(x_prompt.reshape(bsz * seq, D_MODEL), ffn1_w)
    ys_p, qlat_p, qpe_p, kcat_p, ckv_p, kpe_p, st_p = _mixer(
        x1_p.reshape(bsz, seq, D_MODEL), cos_p, sin_p, jnp.zeros((bsz, 2 * GP), F32), mix_w,
        bsz, min(MIX_TT, seq))
    ya_p = _prompt_attn(qlat_p, qpe_p, kcat_p, w_uv_bd, att_g)
    y_p = _ffn2(x1_p, ys_p.reshape(bsz * seq, D_SSM), ya_p.reshape(bsz * seq, D_ATT), ffn2_w)

    n_s = dbs * dseq
    cos_s, sin_s = _rope_tables(past_len + jnp.arange(dseq, dtype=F32))
    x1_s = _ffn1(x_sample.reshape(n_s, D_MODEL), ffn1_w)
    h0_s = jnp.concatenate([state_ssm_re[0].reshape(dbs, GP), state_ssm_im[0].reshape(dbs, GP)], axis=1)
    ys_s, qlat_s, qpe_s, _, ckv_s, kpe_s, st_s = _mixer(
        x1_s.reshape(1, n_s, D_MODEL), jnp.tile(cos_s, (dbs, 1)), jnp.tile(sin_s, (dbs, 1)), h0_s, mix_w,
        dbs, dseq)
    qlat_d = qlat_s.reshape(dbs, dseq, N_HEADS, KV_LORA).transpose(0, 2, 1, 3).reshape(
        dbs, N_HEADS * dseq, KV_LORA)
    qpe_d = qpe_s.reshape(dbs, dseq, 2, N_HEADS, HALF_ROPE).transpose(0, 3, 1, 2, 4).reshape(
        dbs, N_HEADS * dseq, QK_ROPE)
    cnew = ckv_s.reshape(dbs, dseq, KV_LORA)
    penew = kpe_s.reshape(dbs, dseq, QK_ROPE)
    o_d = _decode_attn(page_table, qlat_d, qpe_d, cnew, penew, cache_ckv, cache_kpe.transpose(0, 1, 3, 2))
    o_cat = o_d.reshape(dbs, N_HEADS, dseq, KV_LORA).transpose(0, 2, 1, 3).reshape(n_s, N_HEADS * KV_LORA)
    ya_s = _uv_proj(o_cat, w_uv_bd, att_g)
    y_s = _ffn2(x1_s, ys_s.reshape(n_s, D_SSM), ya_s, ffn2_w)

    state = lambda st, n: (st[:, :GP].reshape(1, n, N_GROUPS, SSM_STATE),
                           st[:, GP:].reshape(1, n, N_GROUPS, SSM_STATE))
    re_p, im_p = state(st_p, bsz)
    re_s, im_s = state(st_s, dbs)
    return (y_p.reshape(bsz, seq, D_MODEL), y_s.reshape(dbs, dseq, D_MODEL),
            ckv_p[None], kpe_p[None], re_p, im_p,
            cnew[None], penew[None], re_s, im_s)
```
